```python
import jax, jax.numpy as jnp
from jax import lax
import numpy as np

D_MODEL = 1024
BATCH = 8
SEQ = 4096
DEPTH = 2
DEC_BATCH = 32
DEC_SEQ = 1
PAST_LEN = 16384
PAGE_SIZE = 128

HEAD_DIM = 64
FOX_WIDTH = 512
FOX_HEADS = FOX_WIDTH // HEAD_DIM
HGRN_DK = 128
HGRN_DV = 128
HGRN_WIDTH = 512
HGRN_HEADS = HGRN_WIDTH // HGRN_DK
HGRN_VWIDTH = HGRN_HEADS * HGRN_DV
MOBA_WIDTH = 512
MOBA_HEADS = MOBA_WIDTH // HEAD_DIM
MOBA_BLOCK = 256
MOBA_TOPK = 3
N_BRANCH = 3
D_FF = 2816
CONV_W = 3
ROPE_THETA = 10000.0
EPS = 1e-6
NEG = -1e30
Q_BLOCK = 128
MOBA_Q_CHUNK = 16
HGRN_CHUNK = 64
IN_SPLITS = (FOX_WIDTH, FOX_WIDTH, FOX_WIDTH, FOX_HEADS,
             HGRN_WIDTH, HGRN_WIDTH, HGRN_VWIDTH, HGRN_VWIDTH,
             MOBA_WIDTH, MOBA_WIDTH, MOBA_WIDTH, N_BRANCH * D_MODEL)
IN_WIDTH = int(sum(IN_SPLITS))
SPLIT_POINTS = tuple(int(v) for v in np.cumsum(IN_SPLITS)[:-1])

kernel_name = 'hybrid_fox_hgrn2_moba_decoder_step'

F32 = jnp.float32


def rmsnorm(x, g):
    xf = x.astype(F32)
    y = xf * lax.rsqrt(jnp.mean(xf * xf, axis=-1, keepdims=True) + EPS)
    return (y * g.astype(F32)).astype(x.dtype)


def rope(x, pos):
    half = x.shape[-1] // 2
    inv = ROPE_THETA ** (-jnp.arange(half, dtype=F32) / half)
    ang = pos.astype(F32)[:, None] * inv[None, :]
    cos = jnp.cos(ang)[None, :, None, :]
    sin = jnp.sin(ang)[None, :, None, :]
    xf = x.astype(F32)
    x1, x2 = xf[..., :half], xf[..., half:]
    return jnp.concatenate([x1 * cos - x2 * sin, x2 * cos + x1 * sin], axis=-1).astype(x.dtype)


def gather_pages(pool, layer, page_table):
    g = pool[layer, page_table]
    return g.reshape(g.shape[0], g.shape[1] * g.shape[2], *g.shape[3:])


def fox_attend(q, cq, qpos, k, v, ck, kpos):
    s = jnp.einsum('bqhd,bkhd->bhqk', q, k).astype(F32) * (HEAD_DIM ** -0.5)
    s = s + cq[..., :, None] - ck[:, :, None, :]
    s = jnp.where(kpos[None, :] <= qpos[:, None], s, NEG)
    p = jax.nn.softmax(s, axis=-1).astype(v.dtype)
    return jnp.einsum('bhqk,bkhd->bqhd', p, v)


def fox_prompt(q, k, v, c, pos):
    B, T, H, d = q.shape
    n = T // Q_BLOCK
    qb = q.reshape(B, n, Q_BLOCK, H, d).swapaxes(0, 1)
    cb = c.reshape(B, H, n, Q_BLOCK).transpose(2, 0, 1, 3)
    pb = pos.reshape(n, Q_BLOCK)
    out = lax.map(lambda a: fox_attend(a[0], a[1], a[2], k, v, c, pos), (qb, cb, pb))
    return out.swapaxes(0, 1).reshape(B, T, H, d)


def moba_blocks(k, v):
    B, L, H, d = k.shape
    nb = -(-L // MOBA_BLOCK)
    pad = ((0, 0), (0, nb * MOBA_BLOCK - L), (0, 0), (0, 0))
    kb = jnp.pad(k, pad).reshape(B, nb, MOBA_BLOCK, H, d).transpose(0, 3, 1, 2, 4)
    vb = jnp.pad(v, pad).reshape(B, nb, MOBA_BLOCK, H, d).transpose(0, 3, 1, 2, 4)
    km = jnp.mean(kb.astype(F32), axis=3).astype(k.dtype)
    return kb, vb, km


def moba_attend(q, qpos, kb, vb, km):
    B, H, NB, BLK, d = kb.shape
    Q = q.shape[1]
    own = qpos // MOBA_BLOCK
    g = jnp.einsum('bqhd,bhnd->bhqn', q, km).astype(F32)
    g = jnp.where(jnp.arange(NB)[None, :] < own[:, None], g, NEG)
    top_g, top_i = lax.top_k(g, min(MOBA_TOPK, NB))
    own_i = jnp.broadcast_to(own[None, None, :, None], (B, H, Q, 1)).astype(top_i.dtype)
    idx = jnp.concatenate([top_i, own_i], axis=-1)
    ok = jnp.concatenate([top_g > 0.5 * NEG, jnp.ones((B, H, Q, 1), dtype=bool)], axis=-1)
    bi = jnp.arange(B)[:, None, None, None]
    hi = jnp.arange(H)[None, :, None, None]
    kg = kb[bi, hi, idx]
    vg = vb[bi, hi, idx]
    kpos = idx[..., None] * MOBA_BLOCK + jnp.arange(BLK)
    mask = ok[..., None] & (kpos <= qpos[None, None, :, None, None])
    s = jnp.einsum('bqhd,bhqjkd->bhqjk', q, kg).astype(F32) * (HEAD_DIM ** -0.5)
    s = jnp.where(mask, s, NEG)
    p = jax.nn.softmax(s.reshape(B, H, Q, -1), axis=-1).reshape(s.shape).astype(vg.dtype)
    return jnp.einsum('bhqjk,bhqjkd->bqhd', p, vg)


def moba_prompt(q, pos, kb, vb, km):
    B, T, H, d = q.shape
    n = T // MOBA_Q_CHUNK
    qc = q.reshape(B, n, MOBA_Q_CHUNK, H, d).swapaxes(0, 1)
    pc = pos.reshape(n, MOBA_Q_CHUNK)
    out = lax.map(lambda a: moba_attend(a[0], a[1], kb, vb, km), (qc, pc))
    return out.swapaxes(0, 1).reshape(B, T, H, d)


def hgrn_scan(q, logf, kin, v, S0):
    B, T, H, dk = q.shape
    C = HGRN_CHUNK if T % HGRN_CHUNK == 0 else T
    n = T // C
    chunk = lambda a: a.reshape(B, n, C, *a.shape[2:]).swapaxes(0, 1)
    causal = jnp.tril(jnp.ones((C, C), dtype=bool))[None, :, :, None, None]

    def step(S, inp):
        qc, gc, kc, vc = inp
        b = jnp.cumsum(gc, axis=1)
        o_inter = jnp.einsum('bthk,bhkv->bthv', qc * jnp.exp(b), S)
        rel = jnp.where(causal, b[:, :, None] - b[:, None, :], NEG)
        A = jnp.sum(qc[:, :, None] * kc[:, None] * jnp.exp(rel), axis=-1)
        o_intra = jnp.einsum('btsh,bshv->bthv', A, vc)
        S = jnp.exp(b[:, -1])[..., None] * S + jnp.einsum('bshk,bshv->bhkv', kc * jnp.exp(b[:, -1:] - b), vc)
        return S, o_inter + o_intra

    S, o = lax.scan(step, S0, (chunk(q), chunk(logf), chunk(kin), chunk(v)))
    return o.swapaxes(0, 1).reshape(B, T, H, -1), S


def conv_ffn(h, prev, w_up, conv_w, conv_b, w_down):
    T = h.shape[1]
    u = h @ w_up
    ext = jnp.concatenate([prev.astype(u.dtype), u], axis=1)
    c = conv_b + conv_w[0] * ext[:, 0:T]
    for j in range(1, CONV_W):
        c = c + conv_w[j] * ext[:, j:j + T]
    a, g = jnp.split(c, 2, axis=-1)
    return (jax.nn.silu(a) * g) @ w_down, ext[:, ext.shape[1] - (CONV_W - 1):]


def trunk_layer(x, pos, past, S0, conv_prev, lb, norm_mix, w_in, fox_fb, fox_qnorm, fox_knorm,
                moba_qnorm, moba_knorm, hgrn_onorm, w_bf, w_bh, w_bm, w_out, norm_ffn,
                ffn_up, conv_w, conv_b, ffn_down):
    B, T, _ = x.shape
    h = rmsnorm(x, norm_mix)
    z = h @ w_in
    fq, fk, fv, ff, hq, hf, hi, hg, mq, mk, mv, gz = jnp.split(z, SPLIT_POINTS, axis=-1)
    heads = lambda a, n: a.reshape(B, T, n, -1)

    fq = rmsnorm(heads(fq, FOX_HEADS), fox_qnorm)
    fk = rmsnorm(heads(fk, FOX_HEADS), fox_knorm)
    fv = heads(fv, FOX_HEADS)
    flogf = jax.nn.log_sigmoid((ff + fox_fb).astype(F32))

    mq = rope(rmsnorm(heads(mq, MOBA_HEADS), moba_qnorm), pos)
    mk = rope(rmsnorm(heads(mk, MOBA_HEADS), moba_knorm), pos)
    mv = heads(mv, MOBA_HEADS)

    if past is None:
        c = jnp.cumsum(flogf, axis=1).swapaxes(1, 2)
        o_fox = fox_prompt(fq, fk, fv, c, pos)
        kb, vb, km = moba_blocks(mk, mv)
        o_moba = moba_prompt(mq, pos, kb, vb, km)
    else:
        pk, pv, pl, pmk, pmv = past
        k_all = jnp.concatenate([pk.astype(fk.dtype), fk], axis=1)
        v_all = jnp.concatenate([pv.astype(fv.dtype), fv], axis=1)
        c = jnp.cumsum(jnp.concatenate([pl.astype(F32), flogf], axis=1), axis=1).swapaxes(1, 2)
        L = k_all.shape[1]
        o_fox = fox_attend(fq, c[:, :, L - T:], pos, k_all, v_all, c, jnp.arange(L))
        kb, vb, km = moba_blocks(jnp.concatenate([pmk.astype(mk.dtype), mk], axis=1),
                                 jnp.concatenate([pmv.astype(mv.dtype), mv], axis=1))
        o_moba = moba_attend(mq, pos, kb, vb, km)

    lbh = lb.reshape(HGRN_HEADS, HGRN_DK)
    zf = heads(hf, HGRN_HEADS).astype(F32)
    f_h = lbh + (1.0 - lbh) * jax.nn.sigmoid(zf)
    logf_h = jnp.log(f_h)
    kin = (1.0 - lbh) * jax.nn.sigmoid(-zf)
    qh = jax.nn.silu(heads(hq, HGRN_HEADS).astype(F32))
    o_h, S_new = hgrn_scan(qh, logf_h, kin, heads(hi, HGRN_HEADS).astype(F32), S0.astype(F32))
    o_h = rmsnorm(o_h, hgrn_onorm) * jax.nn.sigmoid(heads(hg, HGRN_HEADS).astype(F32))

    y_f = o_fox.reshape(B, T, FOX_WIDTH).astype(x.dtype) @ w_bf
    y_h = o_h.reshape(B, T, HGRN_VWIDTH).astype(x.dtype) @ w_bh
    y_m = o_moba.reshape(B, T, MOBA_WIDTH).astype(x.dtype) @ w_bm
    g_f, g_h, g_m = jnp.split(jax.nn.sigmoid(gz), N_BRANCH, axis=-1)
    x = x + (g_f * y_f + g_h * y_h + g_m * y_m) @ w_out

    y_c, conv_new = conv_ffn(rmsnorm(x, norm_ffn), conv_prev, ffn_up, conv_w, conv_b, ffn_down)
    x = x + y_c
    return x, (fk, fv, flogf, mk, mv, S_new, conv_new)


def setup_inputs(seed: int = 0) -> dict:
    key = jax.random.key(seed)
    ks = jax.random.split(key, 40)
    n_pages = PAST_LEN // PAGE_SIZE
    n_used = DEC_BATCH * n_pages
    n_pool = n_used + max(1, n_used // 4)
    nrm = lambda k, shape, s: s * jax.random.normal(k, shape, F32)
    gain = lambda k, shape: 1.0 + 0.02 * jax.random.normal(k, shape, F32)
    page_table = jax.random.permutation(ks[2], n_pool)[:n_used].reshape(DEC_BATCH, n_pages).astype(jnp.int32)
    logf_pre = jax.random.uniform(ks[5], (DEPTH, n_pool, PAGE_SIZE, FOX_HEADS), F32, 1.0, 4.0)
    return {
        'x_prompt': nrm(ks[0], (BATCH, SEQ, D_MODEL), 1.0),
        'x_sample': nrm(ks[1], (DEC_BATCH, DEC_SEQ, D_MODEL), 1.0),
        'page_table': page_table,
        'cache_fox_k': nrm(ks[3], (DEPTH, n_pool, PAGE_SIZE, FOX_HEADS, HEAD_DIM), 1.0),
        'cache_fox_v': nrm(ks[4], (DEPTH, n_pool, PAGE_SIZE, FOX_HEADS, HEAD_DIM), 1.0),
        'cache_fox_logf': jax.nn.log_sigmoid(logf_pre + nrm(ks[6], logf_pre.shape, 0.5)),
        'cache_moba_k': nrm(ks[7], (DEPTH, n_pool, PAGE_SIZE, MOBA_HEADS, HEAD_DIM), 1.0),
        'cache_moba_v': nrm(ks[8], (DEPTH, n_pool, PAGE_SIZE, MOBA_HEADS, HEAD_DIM), 1.0),
        'state_hgrn': nrm(ks[9], (DEPTH, DEC_BATCH, HGRN_HEADS, HGRN_DK, HGRN_DV), 0.5),
        'state_ffn_conv': nrm(ks[10], (DEPTH, DEC_BATCH, CONV_W - 1, 2 * D_FF), 1.0),
        'norm_mix': gain(ks[11], (DEPTH, D_MODEL)),
        'w_in': nrm(ks[12], (DEPTH, D_MODEL, IN_WIDTH), D_MODEL ** -0.5),
        'fox_fb': jax.random.uniform(ks[13], (DEPTH, FOX_HEADS), F32, 1.0, 4.0),
        'fox_qnorm': gain(ks[14], (DEPTH, HEAD_DIM)),
        'fox_knorm': gain(ks[15], (DEPTH, HEAD_DIM)),
        'moba_qnorm': gain(ks[16], (DEPTH, HEAD_DIM)),
        'moba_knorm': gain(ks[17], (DEPTH, HEAD_DIM)),
        'hgrn_lb_logits': nrm(ks[18], (DEPTH, HGRN_WIDTH), 0.5),
        'hgrn_onorm': gain(ks[19], (DEPTH, HGRN_DV)),
        'w_branch_fox': nrm(ks[20], (DEPTH, FOX_WIDTH, D_MODEL), FOX_WIDTH ** -0.5),
        'w_branch_hgrn': nrm(ks[21], (DEPTH, HGRN_VWIDTH, D_MODEL), HGRN_VWIDTH ** -0.5),
        'w_branch_moba': nrm(ks[22], (DEPTH, MOBA_WIDTH, D_MODEL), MOBA_WIDTH ** -0.5),
        'w_out': nrm(ks[23], (DEPTH, D_MODEL, D_MODEL), D_MODEL ** -0.5),
        'norm_ffn': gain(ks[24], (DEPTH, D_MODEL)),
        'ffn_up': nrm(ks[25], (DEPTH, D_MODEL, 2 * D_FF), D_MODEL ** -0.5),
        'ffn_conv_w': nrm(ks[26], (DEPTH, CONV_W, 2 * D_FF), CONV_W ** -0.5),
        'ffn_conv_b': nrm(ks[27], (DEPTH, 2 * D_FF), 0.01),
        'ffn_down': nrm(ks[28], (DEPTH, D_FF, D_MODEL), D_FF ** -0.5),
    }


def reference(x_prompt, x_sample, page_table, cache_fox_k, cache_fox_v, cache_fox_logf,
              cache_moba_k, cache_moba_v, state_hgrn, state_ffn_conv,
              norm_mix, w_in, fox_fb, fox_qnorm, fox_knorm, moba_qnorm, moba_knorm,
              hgrn_lb_logits, hgrn_onorm, w_branch_fox, w_branch_hgrn, w_branch_moba,
              w_out, norm_ffn, ffn_up, ffn_conv_w, ffn_conv_b, ffn_down):
    sm = jax.nn.softmax(hgrn_lb_logits.astype(F32), axis=0)
    lower = jnp.clip(jnp.cumsum(sm, axis=0) - sm[0:1], 0.0, 0.999)
    Bp, Tp = x_prompt.shape[0], x_prompt.shape[1]
    Ts = x_sample.shape[1]
    past_len = page_table.shape[1] * PAGE_SIZE
    pos_p = jnp.arange(Tp)
    pos_s = past_len + jnp.arange(Ts)
    yp, ys = x_prompt, x_sample
    new_p, new_s = [], []
    for l in range(DEPTH):
        w = (lower[l], norm_mix[l], w_in[l], fox_fb[l], fox_qnorm[l], fox_knorm[l],
             moba_qnorm[l], moba_knorm[l], hgrn_onorm[l], w_branch_fox[l], w_branch_hgrn[l],
             w_branch_moba[l], w_out[l], norm_ffn[l], ffn_up[l], ffn_conv_w[l], ffn_conv_b[l], ffn_down[l])
        S0_p = jnp.zeros((Bp, HGRN_HEADS, HGRN_DK, HGRN_DV), F32)
        conv0_p = jnp.zeros((Bp, CONV_W - 1, 2 * D_FF), yp.dtype)
        yp, st_p = trunk_layer(yp, pos_p, None, S0_p, conv0_p, *w)
        new_p.append(st_p)
        past = (gather_pages(cache_fox_k, l, page_table), gather_pages(cache_fox_v, l, page_table),
                gather_pages(cache_fox_logf, l, page_table), gather_pages(cache_moba_k, l, page_table),
                gather_pages(cache_moba_v, l, page_table))
        ys, st_s = trunk_layer(ys, pos_s, past, state_hgrn[l], state_ffn_conv[l], *w)
        new_s.append(st_s)
    p_fox_k, p_fox_v, p_fox_logf, p_moba_k, p_moba_v, p_hgrn, p_conv = [jnp.stack(a) for a in zip(*new_p)]
    s_fox_k, s_fox_v, s_fox_logf, s_moba_k, s_moba_v, s_hgrn, s_conv = [jnp.stack(a) for a in zip(*new_s)]
    return (yp, ys, p_fox_k, p_fox_v, p_fox_logf, p_moba_k, p_moba_v, p_hgrn, p_conv,
            s_fox_k, s_fox_v, s_fox_logf, s_moba_k, s_moba_v, s_hgrn, s_conv)
```

```python
import functools

import numpy as np
import jax
import jax.numpy as jnp
from jax import lax
from jax.experimental import pallas as pl
from jax.experimental.pallas import tpu as pltpu

F32 = jnp.float32
BF16 = jnp.bfloat16

HEAD_DIM = 64
N_HEADS = 8
ATT_W = N_HEADS * HEAD_DIM
HG_HEADS = 4
HG_D = 128
HG_W = HG_HEADS * HG_D
MOBA_BLOCK = 256
MOBA_TOPK = 3
PAGE = 128
N_BRANCH = 3
CONV_W = 3
ROPE_THETA = 10000.0
EPS = 1e-6
NEG = -1e30
REMOVED = -3e38

LANES = 128
SUBLANES = 8
VMEM_LIMIT = 52 * 1024 * 1024

HG_CHUNK = 64
HG_SUB = 16
DEC_PAGES_PER_STEP = 4


def _cp(sem):
    return pltpu.CompilerParams(dimension_semantics=sem, vmem_limit_bytes=VMEM_LIMIT)


def _dot(a, b):
    return jnp.dot(a.astype(BF16), b.astype(BF16), preferred_element_type=F32)


def _dot_nt(a, b):
    return lax.dot_general(a.astype(BF16), b.astype(BF16), (((1,), (1,)), ((), ())),
                           preferred_element_type=F32)


def _dot_tn(a, b):
    return lax.dot_general(a.astype(BF16), b.astype(BF16), (((0,), (0,)), ((), ())),
                           preferred_element_type=F32)


def _pieces(x, n):
    out, r = [], x
    for _ in range(n):
        p = r.astype(BF16)
        out.append(p)
        r = r - p.astype(F32)
    return out


def _rmsnorm(x, g):
    return x * lax.rsqrt(jnp.mean(x * x, axis=-1, keepdims=True) + EPS) * g


def _sigmoid(z):
    return 1.0 / (1.0 + jnp.exp(-z))


def _iota(shape, dim):
    return lax.broadcasted_iota(jnp.int32, shape, dim)


def _attn_proj_kernel(x_ref, g_ref, wfox_ref, wff_ref, fb_ref, wmoba_ref, gm_ref,
                      fqn_ref, fkn_ref, mqn_ref, mkn_ref, cos_ref, sin_ref,
                      fq_ref, fk_ref, fv_ref, lf_ref, mq_ref, mk_ref, mv_ref):
    hb = _rmsnorm(x_ref[...], g_ref[...]).astype(BF16)
    gm = gm_ref[...]

    def headnorm(z, gain):
        hi, lo = _pieces(z * z, 2)
        ms = (jnp.dot(hi, gm, preferred_element_type=F32)
              + jnp.dot(lo, gm, preferred_element_type=F32))
        return z * lax.rsqrt(ms + EPS) * gain

    w = ATT_W
    zf = jnp.dot(hb, wfox_ref[...], preferred_element_type=F32)
    fq_ref[...] = headnorm(zf[:, :w], fqn_ref[...])
    fk_ref[...] = headnorm(zf[:, w:2 * w], fkn_ref[...])
    fv_ref[...] = zf[:, 2 * w:]

    zl = lax.dot_general(wff_ref[...], hb, (((1,), (1,)), ((), ())),
                         preferred_element_type=F32) + fb_ref[...]
    lf_ref[...] = jnp.minimum(zl, 0.0) - jnp.log(1.0 + jnp.exp(-jnp.abs(zl)))

    cos = cos_ref[...]
    sin = sin_ref[...]
    first_half = (_iota((1, w), 1) & (HEAD_DIM - 1)) < HEAD_DIM // 2

    def rope(y):
        swapped = jnp.where(first_half, pltpu.roll(y, w - HEAD_DIM // 2, 1),
                            pltpu.roll(y, HEAD_DIM // 2, 1))
        return y * cos + swapped * sin

    zm = jnp.dot(hb, wmoba_ref[...], preferred_element_type=F32)
    mq_ref[...] = rope(headnorm(zm[:, :w], mqn_ref[...]))
    mk_ref[...] = rope(headnorm(zm[:, w:2 * w], mkn_ref[...]))
    mv_ref[...] = zm[:, 2 * w:]


def _attn_proj(x, gain, wfox, wff, fb, wmoba, gmat, fqn, fkn, mqn, mkn, cos, sin, tm):
    m, d = x.shape
    w = ATT_W
    t_blocks = cos.shape[0] // tm
    row = lambda i: (i, 0)
    const = lambda i: (0, 0)
    big = pl.BlockSpec((tm, w), row)
    out_shape = [jax.ShapeDtypeStruct((m, w), F32)] * 3 + [jax.ShapeDtypeStruct((16, m), F32)] \
        + [jax.ShapeDtypeStruct((m, w), F32)] * 3
    return pl.pallas_call(
        _attn_proj_kernel,
        grid=(m // tm,),
        in_specs=[pl.BlockSpec((tm, d), row), pl.BlockSpec((1, d), const),
                  pl.BlockSpec((d, 3 * w), const), pl.BlockSpec((16, d), const),
                  pl.BlockSpec((16, 1), const), pl.BlockSpec((d, 3 * w), const),
                  pl.BlockSpec((w, w), const)] + [pl.BlockSpec((1, w), const)] * 4
                 + [pl.BlockSpec((tm, w), lambda i: (i % t_blocks, 0))] * 2,
        out_specs=[big, big, big, pl.BlockSpec((16, tm), lambda i: (0, i)), big, big, big],
        out_shape=out_shape,
        compiler_params=_cp(("parallel",)),
        name="attn_proj",
    )(x, gain, wfox, wff, fb, wmoba, gmat, fqn, fkn, mqn, mkn, cos, sin)


def _hgrn_proj_kernel(x_ref, g_ref, w_ref, lb_ref, hq_ref, lf_ref, kin_ref, hi_ref, hg_ref):
    hb = _rmsnorm(x_ref[...], g_ref[...]).astype(BF16)
    z = jnp.dot(hb, w_ref[...], preferred_element_type=F32)
    w = HG_W
    zq = z[:, :w]
    hq_ref[...] = zq * _sigmoid(zq)
    zf = z[:, w:2 * w]
    lb = lb_ref[...]
    lf_ref[...] = jnp.log(lb + (1.0 - lb) * _sigmoid(zf))
    kin_ref[...] = (1.0 - lb) * _sigmoid(-zf)
    hi_ref[...] = z[:, 2 * w:3 * w]
    hg_ref[...] = _sigmoid(z[:, 3 * w:])


def _hgrn_proj(x, gain, w_h, lb, tm):
    m, d = x.shape
    w = HG_W
    row = lambda i: (i, 0)
    const = lambda i: (0, 0)
    return pl.pallas_call(
        _hgrn_proj_kernel,
        grid=(m // tm,),
        in_specs=[pl.BlockSpec((tm, d), row), pl.BlockSpec((1, d), const),
                  pl.BlockSpec((d, 4 * w), const), pl.BlockSpec((1, w), const)],
        out_specs=[pl.BlockSpec((tm, w), row)] * 5,
        out_shape=[jax.ShapeDtypeStruct((m, w), F32)] * 5,
        compiler_params=_cp(("parallel",)),
        name="hgrn_proj",
    )(x, gain, w_h, lb)


def _cumsum_kernel(x_ref, o_ref):
    x = x_ref[...]
    n = x.shape[1]
    lane = _iota(x.shape, 1)
    s = 1
    while s < n:
        x = x + jnp.where(lane >= s, pltpu.roll(x, s, 1), 0.0)
        s *= 2
    o_ref[...] = x


def _cumsum_rows(x, seg):
    r, n = x.shape
    return pl.pallas_call(
        _cumsum_kernel,
        grid=(n // seg,),
        in_specs=[pl.BlockSpec((r, seg), lambda b: (0, b))],
        out_specs=pl.BlockSpec((r, seg), lambda b: (0, b)),
        out_shape=jax.ShapeDtypeStruct((r, n), F32),
        compiler_params=_cp(("parallel",)),
        name="fox_cumsum",
    )(x)


def _pair_tables(n):
    i_tab = np.concatenate([np.full(i + 1, i, np.int32) for i in range(n)])
    j_tab = np.concatenate([np.arange(i + 1, dtype=np.int32) for i in range(n)])
    return jnp.asarray(i_tab), jnp.asarray(j_tab)


def _store_head_queries(q, qs_ref, scale):
    odd = _iota((1, LANES), 1) >= HEAD_DIM
    for h in range(N_HEADS):
        grp = q[:, (h // 2) * LANES:(h // 2 + 1) * LANES] * scale
        keep = odd if h % 2 else jnp.logical_not(odd)
        qs_ref[h] = jnp.where(keep, grp, 0.0).astype(BF16)


def _flash_update(h, s, visible, vb, m_ref, l_ref, acc_ref):
    if visible is not None:
        s = jnp.where(visible, s, NEG)
    m_old = m_ref[h]
    m_new = jnp.maximum(m_old, jnp.max(s, axis=-1, keepdims=True))
    p = jnp.exp(s - m_new[:, :1])
    if visible is not None:
        p = jnp.where(visible, p, 0.0)
    alpha = jnp.exp(m_old - m_new)
    l_ref[h] = alpha * l_ref[h] + jnp.sum(p, axis=-1, keepdims=True)
    acc_ref[h] = alpha * acc_ref[h] + jnp.dot(p.astype(BF16), vb, preferred_element_type=F32)
    m_ref[h] = m_new


def _flash_finish(o_ref, l_ref, acc_ref):
    odd = _iota((1, LANES), 1) >= HEAD_DIM
    for g in range(N_HEADS // 2):
        a0 = acc_ref[2 * g] / l_ref[2 * g]
        a1 = acc_ref[2 * g + 1] / l_ref[2 * g + 1]
        o_ref[:, g * LANES:(g + 1) * LANES] = jnp.where(odd, a1, a0)


def _fox_kernel(itab, jtab, q_ref, k_ref, v_ref, c_ref, o_ref, qs_ref, m_ref, l_ref, acc_ref):
    p = pl.program_id(1)
    i = itab[p]
    j = jtab[p]
    tq = q_ref.shape[0]

    @pl.when(j == 0)
    def _():
        _store_head_queries(q_ref[...], qs_ref, HEAD_DIM ** -0.5)
        m_ref[...] = jnp.full(m_ref.shape, NEG, F32)
        l_ref[...] = jnp.zeros(l_ref.shape, F32)
        acc_ref[...] = jnp.zeros(acc_ref.shape, F32)

    def step(visible):
        for h in range(N_HEADS):
            lanes = slice((h // 2) * LANES, (h // 2 + 1) * LANES)
            kb = k_ref[:, lanes].astype(BF16)
            vb = v_ref[:, lanes].astype(BF16)
            s = _dot_nt(qs_ref[h], kb) - c_ref[h:h + 1, :]
            _flash_update(h, s, visible, vb, m_ref, l_ref, acc_ref)

    @pl.when(j < i)
    def _():
        step(None)

    @pl.when(j == i)
    def _():
        step(_iota((tq, tq), 1) <= _iota((tq, tq), 0))
        _flash_finish(o_ref, l_ref, acc_ref)


def _fox_prompt(q, k, v, c, b, t, tq):
    nq = t // tq
    itab, jtab = _pair_tables(nq)
    w = ATT_W
    grid_spec = pltpu.PrefetchScalarGridSpec(
        num_scalar_prefetch=2,
        grid=(b, int(itab.shape[0])),
        in_specs=[pl.BlockSpec((tq, w), lambda bb, p, it, jt: (bb * nq + it[p], 0)),
                  pl.BlockSpec((tq, w), lambda bb, p, it, jt: (bb * nq + jt[p], 0)),
                  pl.BlockSpec((tq, w), lambda bb, p, it, jt: (bb * nq + jt[p], 0)),
                  pl.BlockSpec((16, tq), lambda bb, p, it, jt: (0, bb * nq + jt[p]))],
        out_specs=pl.BlockSpec((tq, w), lambda bb, p, it, jt: (bb * nq + it[p], 0)),
        scratch_shapes=[pltpu.VMEM((N_HEADS, tq, LANES), BF16),
                        pltpu.VMEM((N_HEADS, tq, LANES), F32),
                        pltpu.VMEM((N_HEADS, tq, LANES), F32),
                        pltpu.VMEM((N_HEADS, tq, LANES), F32)])
    return pl.pallas_call(
        _fox_kernel, grid_spec=grid_spec,
        out_shape=jax.ShapeDtypeStruct((b * t, w), F32),
        compiler_params=_cp(("parallel", "arbitrary")),
        name="fox_prompt",
    )(itab, jtab, q, k, v, c)


def _block_mean_kernel(k_ref, o_ref):
    o_ref[...] = jnp.sum(k_ref[...], axis=0, keepdims=True) * (1.0 / MOBA_BLOCK)


def _block_means(k, nblk):
    w = k.shape[1]
    return pl.pallas_call(
        _block_mean_kernel,
        grid=(nblk,),
        in_specs=[pl.BlockSpec((MOBA_BLOCK, w), lambda n: (n, 0))],
        out_specs=pl.BlockSpec((None, 1, w), lambda n: (n, 0, 0)),
        out_shape=jax.ShapeDtypeStruct((nblk, 1, w), F32),
        compiler_params=_cp(("parallel",)),
        name="moba_block_means",
    )(k)


def _top_blocks(g, n_valid):
    nb = g.shape[1]
    col = _iota(g.shape, 1)
    g = jnp.where(col < n_valid, g, NEG)
    sel = jnp.zeros(g.shape, F32)
    for _ in range(MOBA_TOPK):
        mx = jnp.max(g, axis=-1, keepdims=True)
        idx = jnp.min(jnp.where(g == mx, col, nb), axis=-1, keepdims=True)
        hit = col == idx
        sel = jnp.where(jnp.logical_and(hit, mx > 0.5 * NEG), 1.0, sel)
        g = jnp.where(hit, REMOVED, g)
    return sel


def _moba_kernel(itab, jtab, q_ref, k_ref, v_ref, km_ref, o_ref,
                 qs_ref, sel_ref, m_ref, l_ref, acc_ref):
    p = pl.program_id(1)
    i = itab[p]
    j = jtab[p]
    tq = q_ref.shape[0]

    @pl.when(j == 0)
    def _():
        q = q_ref[...]
        _store_head_queries(q, qs_ref, HEAD_DIM ** -0.5)
        odd = _iota((1, LANES), 1) >= HEAD_DIM
        for h in range(N_HEADS):
            lanes = slice((h // 2) * LANES, (h // 2 + 1) * LANES)
            keep = odd if h % 2 else jnp.logical_not(odd)
            qh, ql = _pieces(jnp.where(keep, q[:, lanes], 0.0), 2)
            kh, kl = _pieces(km_ref[:, lanes], 2)
            g = _dot_nt(qh, kh) + _dot_nt(qh, kl) + _dot_nt(ql, kh)
            sel_ref[h] = _top_blocks(g, i)
        m_ref[...] = jnp.full(m_ref.shape, NEG, F32)
        l_ref[...] = jnp.zeros(l_ref.shape, F32)
        acc_ref[...] = jnp.zeros(acc_ref.shape, F32)

    def step(own_block):
        causal = _iota((tq, tq), 1) <= _iota((tq, tq), 0)
        for h in range(N_HEADS):
            lanes = slice((h // 2) * LANES, (h // 2 + 1) * LANES)
            kb = k_ref[:, lanes].astype(BF16)
            vb = v_ref[:, lanes].astype(BF16)
            s = _dot_nt(qs_ref[h], kb)
            if own_block:
                visible = causal
            else:
                sel = sel_ref[h]
                picked = jnp.sum(jnp.where(_iota(sel.shape, 1) == j, sel, 0.0),
                                 axis=-1, keepdims=True)
                visible = jnp.broadcast_to(picked > 0.5, s.shape)
            _flash_update(h, s, visible, vb, m_ref, l_ref, acc_ref)

    @pl.when(j < i)
    def _():
        step(False)

    @pl.when(j == i)
    def _():
        step(True)
        _flash_finish(o_ref, l_ref, acc_ref)


def _moba_prompt(q, k, v, km, b, t):
    tq = MOBA_BLOCK
    nb = t // tq
    itab, jtab = _pair_tables(nb)
    w = ATT_W
    grid_spec = pltpu.PrefetchScalarGridSpec(
        num_scalar_prefetch=2,
        grid=(b, int(itab.shape[0])),
        in_specs=[pl.BlockSpec((tq, w), lambda bb, p, it, jt: (bb * nb + it[p], 0)),
                  pl.BlockSpec((tq, w), lambda bb, p, it, jt: (bb * nb + jt[p], 0)),
                  pl.BlockSpec((tq, w), lambda bb, p, it, jt: (bb * nb + jt[p], 0)),
                  pl.BlockSpec((nb, w), lambda bb, p, it, jt: (bb, 0))],
        out_specs=pl.BlockSpec((tq, w), lambda bb, p, it, jt: (bb * nb + it[p], 0)),
        scratch_shapes=[pltpu.VMEM((N_HEADS, tq, LANES), BF16),
                        pltpu.VMEM((N_HEADS, tq, nb), F32),
                        pltpu.VMEM((N_HEADS, tq, LANES), F32),
                        pltpu.VMEM((N_HEADS, tq, LANES), F32),
                        pltpu.VMEM((N_HEADS, tq, LANES), F32)])
    return pl.pallas_call(
        _moba_kernel, grid_spec=grid_spec,
        out_shape=jax.ShapeDtypeStruct((b * t, w), F32),
        compiler_params=_cp(("parallel", "arbitrary")),
        name="moba_prompt",
    )(itab, jtab, q, k, v, km)


def _hgrn_kernel(q_ref, f_ref, k_ref, v_ref, g_ref, on_ref, o_ref, s_ref, st_ref, *, n_chunks):
    tstep = pl.program_id(2)
    C, c = HG_CHUNK, HG_SUB

    @pl.when(tstep == 0)
    def _():
        st_ref[...] = jnp.zeros(st_ref.shape, F32)

    ltri = (_iota((C, C), 0) >= _iota((C, C), 1)).astype(BF16)
    trow = _iota((c, HG_D), 0)

    def chunk(ci, carry):
        r0 = pl.multiple_of(ci * C, C)
        q = q_ref[pl.ds(r0, C), :]
        lf = f_ref[pl.ds(r0, C), :]
        kin = k_ref[pl.ds(r0, C), :]
        v = v_ref[pl.ds(r0, C), :]
        b = sum(jnp.dot(ltri, part, preferred_element_type=F32) for part in _pieces(lf, 3))
        st = st_ref[...]
        o_inter = _dot_nt(q * jnp.exp(b), st)
        b_last = b[C - 1:C, :]
        st_ref[...] = st * jnp.exp(b_last) + _dot_tn(v, kin * jnp.exp(b_last - b))

        out = [o_inter[n * c:(n + 1) * c] for n in range(C // c)]
        for jj in range(C // c):
            rows = slice(jj * c, (jj + 1) * c)
            qj, kj, bj, vj = q[rows], kin[rows], b[rows], v[rows]
            acc = jnp.zeros((c, HG_D), F32)
            for s in range(c):
                decay = jnp.exp(jnp.where(trow >= s, bj - bj[s:s + 1], NEG))
                a = jnp.sum(qj * kj[s:s + 1] * decay, axis=-1, keepdims=True)
                acc = acc + a * vj[s:s + 1]
            out[jj] = out[jj] + acc
            if jj + 1 < C // c:
                b_end = bj[c - 1:c]
                kp = kj * jnp.exp(b_end - bj)
                qp = q[(jj + 1) * c:] * jnp.exp(b[(jj + 1) * c:] - b_end)
                oo = _dot(_dot_nt(qp, kp), vj)
                for n in range(jj + 1, C // c):
                    out[n] = out[n] + oo[(n - jj - 1) * c:(n - jj) * c]
        o = jnp.concatenate(out, axis=0)
        o_ref[pl.ds(r0, C), :] = _rmsnorm(o, on_ref[...]) * g_ref[pl.ds(r0, C), :]
        return carry

    lax.fori_loop(0, n_chunks, chunk, 0)

    @pl.when(tstep == pl.num_programs(2) - 1)
    def _():
        s_ref[...] = st_ref[...].T


def _hgrn_prompt(q, lf, kin, v, gate, onorm, b, t, tb):
    nt = t // tb
    blk = pl.BlockSpec((tb, HG_D), lambda bb, h, tt: (bb * nt + tt, h))
    return pl.pallas_call(
        functools.partial(_hgrn_kernel, n_chunks=tb // HG_CHUNK),
        grid=(b, HG_HEADS, nt),
        in_specs=[blk] * 5 + [pl.BlockSpec((1, HG_D), lambda bb, h, tt: (0, 0))],
        out_specs=[blk, pl.BlockSpec((None, None, HG_D, HG_D), lambda bb, h, tt: (bb, h, 0, 0))],
        out_shape=[jax.ShapeDtypeStruct((b * t, HG_W), F32),
                   jax.ShapeDtypeStruct((b, HG_HEADS, HG_D, HG_D), F32)],
        scratch_shapes=[pltpu.VMEM((HG_D, HG_D), F32)],
        compiler_params=_cp(("parallel", "parallel", "arbitrary")),
        name="hgrn_prompt",
    )(q, lf, kin, v, gate, onorm)


def _merge_kernel(x_ref, g_ref, wg_ref, of_ref, oh_ref, om_ref, wf_ref, wh_ref, wm_ref,
                  wo_ref, y_ref):
    x = x_ref[...]
    d = x.shape[1]
    hb = _rmsnorm(x, g_ref[...]).astype(BF16)
    gates = _sigmoid(jnp.dot(hb, wg_ref[...], preferred_element_type=F32))
    y = (gates[:, :d] * _dot(of_ref[...], wf_ref[...])
         + gates[:, d:2 * d] * _dot(oh_ref[...], wh_ref[...])
         + gates[:, 2 * d:] * _dot(om_ref[...], wm_ref[...]))
    y_ref[...] = x + _dot(y, wo_ref[...])


def _merge(x, gain, wg, o_f, o_h, o_m, w_bf, w_bh, w_bm, w_out, tm):
    m, d = x.shape
    w = ATT_W
    row = lambda i: (i, 0)
    const = lambda i: (0, 0)
    return pl.pallas_call(
        _merge_kernel,
        grid=(m // tm,),
        in_specs=[pl.BlockSpec((tm, d), row), pl.BlockSpec((1, d), const),
                  pl.BlockSpec((d, N_BRANCH * d), const)]
                 + [pl.BlockSpec((tm, w), row)] * 3 + [pl.BlockSpec((w, d), const)] * 3
                 + [pl.BlockSpec((d, d), const)],
        out_specs=pl.BlockSpec((tm, d), row),
        out_shape=jax.ShapeDtypeStruct((m, d), F32),
        compiler_params=_cp(("parallel",)),
        name="branch_merge",
    )(x, gain, wg, o_f, o_h, o_m, w_bf, w_bh, w_bm, w_out)


def _ffn_kernel(x_ref, g_ref, wa_ref, wg_ref, cwa_ref, cwg_ref, cba_ref, cbg_ref, wd_ref,
                y_ref, ca_ref, cg_ref, hb_ref, acc_ref, carry_ref, *, tiles_per_seq):
    i = pl.program_id(0)
    f = pl.program_id(1)
    tm = x_ref.shape[0]

    @pl.when(f == 0)
    def _():
        hb_ref[...] = _rmsnorm(x_ref[...], g_ref[...]).astype(BF16)
        acc_ref[...] = jnp.zeros(acc_ref.shape, F32)

    hb = hb_ref[...]
    rowi = _iota((tm, wa_ref.shape[1]), 0)

    @pl.when((i % tiles_per_seq) == 0)
    def _():
        carry_ref[f] = jnp.zeros(carry_ref.shape[1:], F32)

    def conv(u, cw_ref, cb_ref, slot, tail_ref):
        prev = carry_ref[f, slot]
        p1 = prev[SUBLANES - 1:SUBLANES]
        p2 = prev[SUBLANES - 2:SUBLANES - 1]
        u1 = jnp.where(rowi == 0, p1, pltpu.roll(u, 1, 0))
        u2 = jnp.where(rowi == 0, p2, jnp.where(rowi == 1, p1, pltpu.roll(u, 2, 0)))
        cw = cw_ref[...]
        tail = u[tm - SUBLANES:]
        carry_ref[f, slot] = tail
        tail_ref[...] = tail
        return cb_ref[...] + cw[0:1] * u2 + cw[1:2] * u1 + cw[2:3] * u

    a = conv(jnp.dot(hb, wa_ref[...], preferred_element_type=F32), cwa_ref, cba_ref, 0, ca_ref)
    g = conv(jnp.dot(hb, wg_ref[...], preferred_element_type=F32), cwg_ref, cbg_ref, 1, cg_ref)
    acc_ref[...] += _dot(a * _sigmoid(a) * g, wd_ref[...])

    @pl.when(f == pl.num_programs(1) - 1)
    def _():
        y_ref[...] = x_ref[...] + acc_ref[...]


def _ffn_prompt(x, gain, w_up, conv_w, conv_b, w_down, b, t, tm, fc):
    m, d = x.shape
    dff = w_down.shape[0]
    nf = dff // fc
    tiles = t // tm
    return pl.pallas_call(
        functools.partial(_ffn_kernel, tiles_per_seq=tiles),
        grid=(m // tm, nf),
        in_specs=[pl.BlockSpec((tm, d), lambda i, f: (i, 0)),
                  pl.BlockSpec((1, d), lambda i, f: (0, 0)),
                  pl.BlockSpec((d, fc), lambda i, f: (0, f)),
                  pl.BlockSpec((d, fc), lambda i, f: (0, nf + f)),
                  pl.BlockSpec((CONV_W, fc), lambda i, f: (0, f)),
                  pl.BlockSpec((CONV_W, fc), lambda i, f: (0, nf + f)),
                  pl.BlockSpec((1, fc), lambda i, f: (0, f)),
                  pl.BlockSpec((1, fc), lambda i, f: (0, nf + f)),
                  pl.BlockSpec((fc, d), lambda i, f: (f, 0))],
        out_specs=[pl.BlockSpec((tm, d), lambda i, f: (i, 0)),
                   pl.BlockSpec((None, SUBLANES, fc), lambda i, f: (i, 0, f)),
                   pl.BlockSpec((None, SUBLANES, fc), lambda i, f: (i, 0, f))],
        out_shape=[jax.ShapeDtypeStruct((m, d), F32),
                   jax.ShapeDtypeStruct((m // tm, SUBLANES, dff), F32),
                   jax.ShapeDtypeStruct((m // tm, SUBLANES, dff), F32)],
        scratch_shapes=[pltpu.VMEM((tm, d), BF16), pltpu.VMEM((tm, d), F32),
                        pltpu.VMEM((nf, 2, SUBLANES, fc), F32)],
        compiler_params=_cp(("arbitrary", "arbitrary")),
        name="conv_ffn_prompt",
    )(x, gain, w_up, w_up, conv_w, conv_w, conv_b, conv_b, w_down)


def _ffn_step_kernel(x_ref, g_ref, wa_ref, wg_ref, cwa_ref, cwg_ref, cba_ref, cbg_ref,
                     pa0_ref, pa1_ref, pg0_ref, pg1_ref, wd_ref, y_ref, ua_ref, ug_ref,
                     hb_ref, acc_ref):
    f = pl.program_id(0)

    @pl.when(f == 0)
    def _():
        hb_ref[...] = _rmsnorm(x_ref[...], g_ref[...]).astype(BF16)
        acc_ref[...] = jnp.zeros(acc_ref.shape, F32)

    hb = hb_ref[...]

    def conv(w_ref, cw_ref, cb_ref, p0_ref, p1_ref, u_ref):
        u = jnp.dot(hb, w_ref[...], preferred_element_type=F32)
        u_ref[...] = u
        cw = cw_ref[...]
        return cb_ref[...] + cw[0:1] * p0_ref[...] + cw[1:2] * p1_ref[...] + cw[2:3] * u

    a = conv(wa_ref, cwa_ref, cba_ref, pa0_ref, pa1_ref, ua_ref)
    g = conv(wg_ref, cwg_ref, cbg_ref, pg0_ref, pg1_ref, ug_ref)
    acc_ref[...] += _dot(a * _sigmoid(a) * g, wd_ref[...])

    @pl.when(f == pl.num_programs(0) - 1)
    def _():
        y_ref[...] = x_ref[...] + acc_ref[...]


def _ffn_step(x, gain, w_up, conv_w, conv_b, w_down, prev0, prev1, fc):
    m, d = x.shape
    dff = w_down.shape[0]
    nf = dff // fc
    lo = lambda f: (0, f)
    hi = lambda f: (0, nf + f)
    return pl.pallas_call(
        _ffn_step_kernel,
        grid=(nf,),
        in_specs=[pl.BlockSpec((m, d), lambda f: (0, 0)), pl.BlockSpec((1, d), lambda f: (0, 0)),
                  pl.BlockSpec((d, fc), lo), pl.BlockSpec((d, fc), hi),
                  pl.BlockSpec((CONV_W, fc), lo), pl.BlockSpec((CONV_W, fc), hi),
                  pl.BlockSpec((1, fc), lo), pl.BlockSpec((1, fc), hi),
                  pl.BlockSpec((m, fc), lo), pl.BlockSpec((m, fc), lo),
                  pl.BlockSpec((m, fc), hi), pl.BlockSpec((m, fc), hi),
                  pl.BlockSpec((fc, d), lambda f: (f, 0))],
        out_specs=[pl.BlockSpec((m, d), lambda f: (0, 0)),
                   pl.BlockSpec((m, fc), lo), pl.BlockSpec((m, fc), lo)],
        out_shape=[jax.ShapeDtypeStruct((m, d), F32), jax.ShapeDtypeStruct((m, dff), F32),
                   jax.ShapeDtypeStruct((m, dff), F32)],
        scratch_shapes=[pltpu.VMEM((m, d), BF16), pltpu.VMEM((m, d), F32)],
        compiler_params=_cp(("arbitrary",)),
        name="conv_ffn_step",
    )(x, gain, w_up, w_up, conv_w, conv_w, conv_b, conv_b, prev0, prev1, prev0, prev1, w_down)


def _decode_scan_kernel(pt_ref, fq_ref, mq_ref, lfn_ref, kn_ref, vn_ref, *rest,
                        pages_per_step, n_blocks):
    pp = pages_per_step
    fk = rest[0:pp]
    fv = rest[pp:2 * pp]
    lf = rest[2 * pp:3 * pp]
    mk = rest[3 * pp:4 * pp]
    o_ref, top_ref, m_ref, l_ref, acc_ref, carry_ref, kms_ref = rest[4 * pp:]
    g = pl.program_id(1)
    half = PAGE // 2

    @pl.when(g == 0)
    def _():
        m_ref[...] = jnp.full(m_ref.shape, NEG, F32)
        l_ref[...] = jnp.zeros(l_ref.shape, F32)
        acc_ref[...] = jnp.zeros(acc_ref.shape, F32)
        carry_ref[...] = jnp.zeros(carry_ref.shape, F32)

    q3 = fq_ref[...] * (HEAD_DIM ** -0.5)
    utri = (_iota((PAGE, PAGE), 0) <= _iota((PAGE, PAGE), 1)).astype(BF16)
    zeros8 = jnp.zeros((SUBLANES, PAGE), F32)
    own_lane = _iota((half, N_HEADS, HEAD_DIM), 0) == _iota((half, N_HEADS, HEAD_DIM), 2)

    def attend(z3, v3):
        m_old = m_ref[...]
        m_new = jnp.maximum(m_old, jnp.max(z3, axis=0))
        p3 = jnp.exp(z3 - m_new)
        alpha = jnp.exp(m_old - m_new)
        l_ref[...] = alpha * l_ref[...] + jnp.sum(p3, axis=0)
        acc_ref[...] = alpha * acc_ref[...] + jnp.sum(p3 * v3, axis=0)
        m_ref[...] = m_new

    for r in range(pp):
        parts = [part.astype(F32) for part in _pieces(lf[r][...].T, 3)]
        stacked = jnp.concatenate(parts + [zeros8], axis=0).astype(BF16)
        y = jnp.dot(stacked, utri, preferred_element_type=F32)
        c_t = carry_ref[...] + y[0:8] + y[8:16] + y[16:24]
        carry_ref[...] = jnp.broadcast_to(c_t[:, PAGE - 1:PAGE], c_t.shape)
        zs = []
        for hv in range(2):
            c3 = jnp.where(own_lane, c_t[:, hv * half:(hv + 1) * half][None], 0.0)
            k3 = fk[r][hv * half:(hv + 1) * half]
            zs.append(jnp.sum(k3 * q3 - c3, axis=-1, keepdims=True))
        attend(jnp.concatenate(zs, axis=0), fv[r][...])

    for r in range(0, pp, 2):
        kms_ref[g * (pp // 2) + r // 2] = jnp.sum(mk[r][...], axis=0) + jnp.sum(mk[r + 1][...], axis=0)

    @pl.when(g == pl.num_programs(1) - 1)
    def _():
        c_new = carry_ref[:, 0:1] + lfn_ref[:, 0:1]
        z_new = jnp.sum(kn_ref[...] * q3, axis=-1, keepdims=True) - c_new
        attend(z_new[None], vn_ref[...][None])
        o_ref[...] = acc_ref[...] / l_ref[...]

        gate = jnp.sum(kms_ref[...] * (mq_ref[...] * (1.0 / MOBA_BLOCK)), axis=-1, keepdims=True)
        blk = _iota(gate.shape, 0)
        lane = _iota((SUBLANES, PAGE), 1)
        top = jnp.zeros((SUBLANES, PAGE), jnp.int32)
        for r in range(MOBA_TOPK):
            mx = jnp.max(gate, axis=0)
            idx = jnp.min(jnp.where(gate == mx, blk, n_blocks), axis=0)
            top = jnp.where(lane == r, idx, top)
            gate = jnp.where(blk == idx, REMOVED, gate)
        top_ref[...] = top


def _decode_scan(page_table, layer, fq, mq, lf_new, k_new, v_new,
                 cache_fk, cache_fv, cache_lf, cache_mk):
    db, n_pages = page_table.shape
    pp = DEC_PAGES_PER_STEP
    n_blocks = n_pages * PAGE // MOBA_BLOCK
    seq = lambda cols: pl.BlockSpec((None, N_HEADS, cols), lambda b, g, pt: (b, 0, 0))

    def page(r):
        return pl.BlockSpec((None, None, PAGE, N_HEADS, HEAD_DIM),
                            lambda b, g, pt, r=r: (layer, pt[b, g * pp + r], 0, 0, 0))

    def gate_page(r):
        return pl.BlockSpec((None, None, PAGE, N_HEADS),
                            lambda b, g, pt, r=r: (layer, pt[b, g * pp + r], 0, 0))

    in_specs = [seq(HEAD_DIM), seq(HEAD_DIM), seq(PAGE), seq(HEAD_DIM), seq(HEAD_DIM)]
    in_specs += [page(r) for r in range(pp)] * 2 + [gate_page(r) for r in range(pp)]
    in_specs += [page(r) for r in range(pp)]
    grid_spec = pltpu.PrefetchScalarGridSpec(
        num_scalar_prefetch=1, grid=(db, n_pages // pp), in_specs=in_specs,
        out_specs=[seq(HEAD_DIM), seq(PAGE)],
        scratch_shapes=[pltpu.VMEM((N_HEADS, 1), F32), pltpu.VMEM((N_HEADS, 1), F32),
                        pltpu.VMEM((N_HEADS, HEAD_DIM), F32), pltpu.VMEM((N_HEADS, PAGE), F32),
                        pltpu.VMEM((n_blocks, N_HEADS, HEAD_DIM), F32)])
    return pl.pallas_call(
        functools.partial(_decode_scan_kernel, pages_per_step=pp, n_blocks=n_blocks),
        grid_spec=grid_spec,
        out_shape=[jax.ShapeDtypeStruct((db, N_HEADS, HEAD_DIM), F32),
                   jax.ShapeDtypeStruct((db, N_HEADS, PAGE), jnp.int32)],
        compiler_params=_cp(("parallel", "arbitrary")),
        name="decode_scan",
    )(page_table, fq, mq, lf_new, k_new, v_new,
      *([cache_fk] * pp), *([cache_fv] * pp), *([cache_lf] * pp), *([cache_mk] * pp))


def _moba_step_kernel(pt_ref, top_ref, q_ref, kn_ref, vn_ref, *rest):
    n = 2 * MOBA_TOPK
    kp = rest[0:n]
    vp = rest[n:2 * n]
    o_ref = rest[2 * n]
    q8 = jnp.broadcast_to(q_ref[...] * (HEAD_DIM ** -0.5), (SUBLANES, HEAD_DIM))
    s_own = jnp.sum(q8 * kn_ref[...], axis=-1, keepdims=True)

    def one_head(hh):
        scores = [_dot_nt(q8, kp[r][:, hh, :]) for r in range(n)]
        m = s_own
        for s in scores:
            m = jnp.maximum(m, jnp.max(s, axis=-1, keepdims=True))
        p_own = jnp.exp(s_own - m)
        l = p_own
        acc = p_own * vn_ref[...]
        for r, s in enumerate(scores):
            p = jnp.exp(s - m)
            l = l + jnp.sum(p, axis=-1, keepdims=True)
            acc = acc + _dot(p, vp[r][:, hh, :])
        o_ref[...] = (acc / l)[0:1]

    for hh in range(N_HEADS):
        pl.when(pl.program_id(1) == hh)(functools.partial(one_head, hh))


def _moba_step(page_table, top_flat, layer, mq, k_new, v_new, cache_mk, cache_mv):
    db = page_table.shape[0]
    vec = pl.BlockSpec((None, None, 1, HEAD_DIM), lambda b, h, pt, tp: (b, h, 0, 0))

    def page(r, half):
        def index(b, h, pt, tp):
            blk = tp[(b * N_HEADS + h) * MOBA_TOPK + r]
            return (layer, pt[b, 2 * blk + half], 0, 0, 0)
        return pl.BlockSpec((None, None, PAGE, N_HEADS, HEAD_DIM), index)

    pages = [page(r, half) for r in range(MOBA_TOPK) for half in range(2)]
    grid_spec = pltpu.PrefetchScalarGridSpec(
        num_scalar_prefetch=2, grid=(db, N_HEADS),
        in_specs=[vec, vec, vec] + pages + pages,
        out_specs=vec)
    n = len(pages)
    return pl.pallas_call(
        _moba_step_kernel, grid_spec=grid_spec,
        out_shape=jax.ShapeDtypeStruct((db, N_HEADS, 1, HEAD_DIM), F32),
        compiler_params=_cp(("parallel", "parallel")),
        name="moba_step",
    )(page_table, top_flat, mq, k_new, v_new, *([cache_mk] * n), *([cache_mv] * n))


def _hgrn_step_kernel(q_ref, f_ref, k_ref, v_ref, g_ref, on_ref, s0_ref, o_ref, s_ref):
    def column(row):
        return jnp.broadcast_to(row, (HG_D, HG_D)).T

    s_new = column(jnp.exp(f_ref[...])) * s0_ref[...] + column(k_ref[...]) * v_ref[...]
    s_ref[...] = s_new
    o = jnp.sum(column(q_ref[...]) * s_new, axis=0, keepdims=True)
    o_ref[...] = _rmsnorm(o, on_ref[...]) * g_ref[...]


def _hgrn_step(q, lf, kin, v, gate, onorm, s0):
    db = q.shape[0]
    vec = pl.BlockSpec((None, 1, HG_D), lambda b, h: (b, 0, h))
    mat = pl.BlockSpec((None, None, HG_D, HG_D), lambda b, h: (b, h, 0, 0))
    return pl.pallas_call(
        _hgrn_step_kernel,
        grid=(db, HG_HEADS),
        in_specs=[vec] * 5 + [pl.BlockSpec((1, HG_D), lambda b, h: (0, 0)), mat],
        out_specs=[vec, mat],
        out_shape=[jax.ShapeDtypeStruct((db, 1, HG_W), F32), jax.ShapeDtypeStruct(s0.shape, F32)],
        compiler_params=_cp(("parallel", "parallel")),
        name="hgrn_step",
    )(q, lf, kin, v, gate, onorm, s0)


def _rope_tables(pos):
    half = HEAD_DIM // 2
    inv = ROPE_THETA ** (-jnp.arange(half, dtype=F32) / half)
    ang = pos.astype(F32)[:, None] * inv[None, :]
    cos = jnp.tile(jnp.cos(ang), (1, 2 * N_HEADS))
    sin = jnp.sin(ang)
    sin = jnp.tile(jnp.concatenate([-sin, sin], axis=1), (1, N_HEADS))
    return cos, sin


def _row_tile(m, want):
    return want if m % want == 0 else m


def kernel(x_prompt, x_sample, page_table, cache_fox_k, cache_fox_v, cache_fox_logf,
           cache_moba_k, cache_moba_v, state_hgrn, state_ffn_conv,
           norm_mix, w_in, fox_fb, fox_qnorm, fox_knorm, moba_qnorm, moba_knorm,
           hgrn_lb_logits, hgrn_onorm, w_branch_fox, w_branch_hgrn, w_branch_moba,
           w_out, norm_ffn, ffn_up, ffn_conv_w, ffn_conv_b, ffn_down):
    depth = w_in.shape[0]
    bp, tp, d = x_prompt.shape
    db, ts, _ = x_sample.shape
    n_pages = page_table.shape[1]
    past_len = n_pages * PAGE
    dff = ffn_down.shape[1]
    w = ATT_W

    sm = jax.nn.softmax(hgrn_lb_logits.astype(F32), axis=0)
    lower = jnp.clip(jnp.cumsum(sm, axis=0) - sm[0:1], 0.0, 0.999)

    head_of = np.arange(w) // HEAD_DIM
    gmat = jnp.asarray((head_of[:, None] == head_of[None, :]) / HEAD_DIM, BF16)

    cos_p, sin_p = _rope_tables(jnp.arange(tp))
    cos_s, sin_s = _rope_tables(jnp.full((db * ts,), past_len) + jnp.tile(jnp.arange(ts), db))

    xp = x_prompt.reshape(bp * tp, d)
    xs = x_sample.reshape(db * ts, d)
    mp, ms = xp.shape[0], xs.shape[0]
    new_p, new_s = [], []
    tile = lambda v: jnp.tile(v, N_HEADS)[None, :]

    for l in range(depth):
        wl = w_in[l]
        o0 = 0
        wfox = wl[:, o0:o0 + 3 * w].astype(BF16); o0 += 3 * w
        wff = jnp.zeros((16, d), BF16).at[:N_HEADS].set(wl[:, o0:o0 + N_HEADS].T.astype(BF16)); o0 += N_HEADS
        whg = wl[:, o0:o0 + 4 * HG_W].astype(BF16); o0 += 4 * HG_W
        wmoba = wl[:, o0:o0 + 3 * w].astype(BF16); o0 += 3 * w
        wgate = wl[:, o0:].astype(BF16)
        fb = jnp.zeros((16, 1), F32).at[:N_HEADS, 0].set(fox_fb[l])
        gmix = norm_mix[l][None, :]
        lb = lower[l][None, :]
        onorm = hgrn_onorm[l][None, :]
        norms = (tile(fox_qnorm[l]), tile(fox_knorm[l]), tile(moba_qnorm[l]), tile(moba_knorm[l]))
        w_bf, w_bh, w_bm = (a[l].astype(BF16) for a in (w_branch_fox, w_branch_hgrn, w_branch_moba))
        w_o = w_out[l].astype(BF16)
        gffn = norm_ffn[l][None, :]
        up = ffn_up[l].astype(BF16)
        down = ffn_down[l].astype(BF16)
        cw = ffn_conv_w[l]
        cb = ffn_conv_b[l][None, :]

        tm = _row_tile(tp, 512)
        fq, fk, fv, lft, mq, mk, mv = _attn_proj(xp, gmix, wfox, wff, fb, wmoba, gmat, *norms,
                                                 cos_p, sin_p, tm)
        hq, hlf, hkin, hi, hg = _hgrn_proj(xp, gmix, whg, lb, tm)
        c = _cumsum_rows(lft, tp)
        o_f = _fox_prompt(fq, fk, fv, c, bp, tp, MOBA_BLOCK)
        nb = tp // MOBA_BLOCK
        km = _block_means(mk, bp * nb).reshape(bp * nb, w)
        o_m = _moba_prompt(mq, mk, mv, km, bp, tp)
        o_h, s_p = _hgrn_prompt(hq, hlf, hkin, hi, hg, onorm, bp, tp, _row_tile(tp, 512))
        xp = _merge(xp, gmix, wgate, o_f, o_h, o_m, w_bf, w_bh, w_bm, w_o, tm)
        xp, ta, tg = _ffn_prompt(xp, gffn, up, cw, cb, down, bp, tp, _row_tile(tp, 1024), 256)
        tiles = tp // _row_tile(tp, 1024)
        conv_p = jnp.concatenate([ta, tg], axis=-1)[tiles - 1::tiles, SUBLANES - (CONV_W - 1):]
        new_p.append((fk.reshape(bp, tp, N_HEADS, HEAD_DIM), fv.reshape(bp, tp, N_HEADS, HEAD_DIM),
                      lft[:N_HEADS].T.reshape(bp, tp, N_HEADS),
                      mk.reshape(bp, tp, N_HEADS, HEAD_DIM), mv.reshape(bp, tp, N_HEADS, HEAD_DIM),
                      s_p, conv_p))

        fq, fk, fv, lft, mq, mk, mv = _attn_proj(xs, gmix, wfox, wff, fb, wmoba, gmat, *norms,
                                                 cos_s, sin_s, ms)
        hq, hlf, hkin, hi, hg = _hgrn_proj(xs, gmix, whg, lb, ms)
        r3 = lambda a: a.reshape(db, 1, a.shape[-1])
        hd = lambda a: a.reshape(db, N_HEADS, HEAD_DIM)
        hd1 = lambda a: a.reshape(db, N_HEADS, 1, HEAD_DIM)
        lf_new = jnp.broadcast_to(lft[:N_HEADS].T[:, :, None], (db, N_HEADS, PAGE))
        o_f, top = _decode_scan(page_table, l, hd(fq), hd(mq), lf_new, hd(fk), hd(fv),
                                cache_fox_k, cache_fox_v, cache_fox_logf, cache_moba_k)
        top_flat = top[:, :, :MOBA_TOPK].reshape(-1)
        o_m = _moba_step(page_table, top_flat, l, hd1(mq), hd1(mk), hd1(mv),
                         cache_moba_k, cache_moba_v)
        o_h, s_s = _hgrn_step(r3(hq), r3(hlf), r3(hkin), r3(hi), r3(hg), onorm, state_hgrn[l])
        xs = _merge(xs, gmix, wgate, o_f.reshape(ms, w), o_h.reshape(ms, HG_W), o_m.reshape(ms, w),
                    w_bf, w_bh, w_bm, w_o, ms)
        prev = state_ffn_conv[l]
        xs, ua, ug = _ffn_step(xs, gffn, up, cw, cb, down, prev[:, 0], prev[:, 1], 256)
        conv_s = jnp.stack([prev[:, 1], jnp.concatenate([ua, ug], axis=-1)], axis=1)
        new_s.append((fk.reshape(db, ts, N_HEADS, HEAD_DIM), fv.reshape(db, ts, N_HEADS, HEAD_DIM),
                      lft[:N_HEADS].T.reshape(db, ts, N_HEADS),
                      mk.reshape(db, ts, N_HEADS, HEAD_DIM), mv.reshape(db, ts, N_HEADS, HEAD_DIM),
                      s_s, conv_s))

    outs_p = [jnp.stack(a) for a in zip(*new_p)]
    outs_s = [jnp.stack(a) for a in zip(*new_s)]
    return (xp.reshape(bp, tp, d), xs.reshape(db, ts, d), *outs_p, *outs_s)
```

```python
import functools

import numpy as np
import jax
import jax.numpy as jnp
from jax import lax
from jax.experimental import pallas as pl
from jax.experimental.pallas import tpu as pltpu

F32 = jnp.float32
BF16 = jnp.bfloat16

HEAD_DIM = 64
N_HEADS = 8
ATT_W = N_HEADS * HEAD_DIM
HG_HEADS = 4
HG_D = 128
HG_W = HG_HEADS * HG_D
MOBA_BLOCK = 256
MOBA_TOPK = 3
PAGE = 128
N_BRANCH = 3
CONV_W = 3
ROPE_THETA = 10000.0
EPS = 1e-6
NEG = -1e30
REMOVED = -3e38

LANES = 128
SUBLANES = 8
VMEM_LIMIT = 52 * 1024 * 1024

HG_CHUNK = 64
HG_SUB = 16
DEC_PAGES_PER_STEP = 4


def _cp(sem):
    return pltpu.CompilerParams(dimension_semantics=sem, vmem_limit_bytes=VMEM_LIMIT)


def _dot(a, b):
    return jnp.dot(a.astype(BF16), b.astype(BF16), preferred_element_type=F32)


def _dot_nt(a, b):
    return lax.dot_general(a.astype(BF16), b.astype(BF16), (((1,), (1,)), ((), ())),
                           preferred_element_type=F32)


def _dot_tn(a, b):
    return lax.dot_general(a.astype(BF16), b.astype(BF16), (((0,), (0,)), ((), ())),
                           preferred_element_type=F32)


def _pieces(x, n):
    out, r = [], x
    for _ in range(n):
        p = r.astype(BF16)
        out.append(p)
        r = r - p.astype(F32)
    return out


def _rmsnorm(x, g):
    return x * lax.rsqrt(jnp.mean(x * x, axis=-1, keepdims=True) + EPS) * g


def _sigmoid(z):
    return 1.0 / (1.0 + jnp.exp(-z))


def _iota(shape, dim):
    return lax.broadcasted_iota(jnp.int32, shape, dim)


def _attn_proj_kernel(x_ref, g_ref, wfox_ref, wff_ref, fb_ref, wmoba_ref, gm_ref,
                      fqn_ref, fkn_ref, mqn_ref, mkn_ref, cos_ref, sin_ref,
                      fq_ref, fk_ref, fv_ref, lf_ref, mq_ref, mk_ref, mv_ref):
    hb = _rmsnorm(x_ref[...], g_ref[...]).astype(BF16)
    gm = gm_ref[...]

    def headnorm(z, gain):
        hi, lo = _pieces(z * z, 2)
        ms = (jnp.dot(hi, gm, preferred_element_type=F32)
              + jnp.dot(lo, gm, preferred_element_type=F32))
        return z * lax.rsqrt(ms + EPS) * gain

    w = ATT_W
    zf = jnp.dot(hb, wfox_ref[...], preferred_element_type=F32)
    fq_ref[...] = headnorm(zf[:, :w], fqn_ref[...])
    fk_ref[...] = headnorm(zf[:, w:2 * w], fkn_ref[...])
    fv_ref[...] = zf[:, 2 * w:]

    zl = lax.dot_general(wff_ref[...], hb, (((1,), (1,)), ((), ())),
                         preferred_element_type=F32) + fb_ref[...]
    lf_ref[...] = jnp.minimum(zl, 0.0) - jnp.log(1.0 + jnp.exp(-jnp.abs(zl)))

    cos = cos_ref[...]
    sin = sin_ref[...]
    first_half = (_iota((1, w), 1) & (HEAD_DIM - 1)) < HEAD_DIM // 2

    def rope(y):
        swapped = jnp.where(first_half, pltpu.roll(y, w - HEAD_DIM // 2, 1),
                            pltpu.roll(y, HEAD_DIM // 2, 1))
        return y * cos + swapped * sin

    zm = jnp.dot(hb, wmoba_ref[...], preferred_element_type=F32)
    mq_ref[...] = rope(headnorm(zm[:, :w], mqn_ref[...]))
    mk_ref[...] = rope(headnorm(zm[:, w:2 * w], mkn_ref[...]))
    mv_ref[...] = zm[:, 2 * w:]


def _attn_proj(x, gain, wfox, wff, fb, wmoba, gmat, fqn, fkn, mqn, mkn, cos, sin, tm):
    m, d = x.shape
    w = ATT_W
    t_blocks = cos.shape[0] // tm
    row = lambda i: (i, 0)
    const = lambda i: (0, 0)
    big = pl.BlockSpec((tm, w), row)
    out_shape = [jax.ShapeDtypeStruct((m, w), F32)] * 3 + [jax.ShapeDtypeStruct((16, m), F32)] \
        + [jax.ShapeDtypeStruct((m, w), F32)] * 3
    return pl.pallas_call(
        _attn_proj_kernel,
        grid=(m // tm,),
        in_specs=[pl.BlockSpec((tm, d), row), pl.BlockSpec((1, d), const),
                  pl.BlockSpec((d, 3 * w), const), pl.BlockSpec((16, d), const),
                  pl.BlockSpec((16, 1), const), pl.BlockSpec((d, 3 * w), const),
                  pl.BlockSpec((w, w), const)] + [pl.BlockSpec((1, w), const)] * 4
                 + [pl.BlockSpec((tm, w), lambda i: (i % t_blocks, 0))] * 2,
        out_specs=[big, big, big, pl.BlockSpec((16, tm), lambda i: (0, i)), big, big, big],
        out_shape=out_shape,
        compiler_params=_cp(("parallel",)),
        name="attn_proj",
    )(x, gain, wfox, wff, fb, wmoba, gmat, fqn, fkn, mqn, mkn, cos, sin)


def _attn_proj_t_kernel(x_ref, g_ref, wq_ref, wkv_ref, wff_ref, fb_ref, gm_ref,
                        fqn_ref, mqn_ref, fkn_ref, mkn_ref, cos_ref, sin_ref, cost_ref, sint_ref,
                        fq_ref, mq_ref, lf_ref, fk_ref, fv_ref, mk_ref, mv_ref):
    hb = _rmsnorm(x_ref[...], g_ref[...]).astype(BF16)
    gm = gm_ref[...]
    w = ATT_W
    half = HEAD_DIM // 2

    def headnorm(z, gain):
        hi, lo = _pieces(z * z, 2)
        ms = (jnp.dot(hi, gm, preferred_element_type=F32)
              + jnp.dot(lo, gm, preferred_element_type=F32))
        return z * lax.rsqrt(ms + EPS) * gain

    first_half = (_iota((1, w), 1) & (HEAD_DIM - 1)) < half

    def rope(y):
        swapped = jnp.where(first_half, pltpu.roll(y, w - half, 1), pltpu.roll(y, half, 1))
        return y * cos_ref[...] + swapped * sin_ref[...]

    zq = jnp.dot(hb, wq_ref[...], preferred_element_type=F32)
    fq_ref[...] = headnorm(zq[:, :w], fqn_ref[...])
    mq_ref[...] = rope(headnorm(zq[:, w:], mqn_ref[...]))

    nt = (((1,), (1,)), ((), ()))
    zl = lax.dot_general(wff_ref[...], hb, nt, preferred_element_type=F32) + fb_ref[...]
    lf_ref[...] = jnp.minimum(zl, 0.0) - jnp.log(1.0 + jnp.exp(-jnp.abs(zl)))

    def headnorm_t(z, gain):
        z3 = z.reshape(N_HEADS, HEAD_DIM, z.shape[1])
        ms = jnp.mean(z3 * z3, axis=1, keepdims=True)
        gain_wide = jnp.concatenate([gain] * (z.shape[1] // LANES), axis=1)
        return z3 * lax.rsqrt(ms + EPS) * gain_wide[None]

    def rope_t(y3):
        x1, x2 = y3[:, :half], y3[:, half:]
        cos, sin = cost_ref[...][None], sint_ref[...][None]
        return jnp.concatenate([x1 * cos - x2 * sin, x2 * cos + x1 * sin], axis=1)

    tm = hb.shape[0]
    zkv = lax.dot_general(wkv_ref[...], hb, nt, preferred_element_type=F32)
    fk_ref[...] = headnorm_t(zkv[:w], fkn_ref[...]).reshape(w, tm)
    fv_ref[...] = zkv[w:2 * w]
    mk_ref[...] = rope_t(headnorm_t(zkv[2 * w:3 * w], mkn_ref[...])).reshape(w, tm)
    mv_ref[...] = zkv[3 * w:]


def _attn_proj_t(x, gain, wq, wkv_t, wff, fb, gmat, fqn, mqn, fkn_col, mkn_col,
                 cos, sin, cos_t, sin_t, b, t, tm):
    m, d = x.shape
    w = ATT_W
    tiles = t // tm
    row = lambda i: (i, 0)
    const = lambda i: (0, 0)
    nat = pl.BlockSpec((tm, w), row)
    tr = pl.BlockSpec((None, w, tm), lambda i: (i // tiles, 0, i % tiles))
    return pl.pallas_call(
        _attn_proj_t_kernel,
        grid=(m // tm,),
        in_specs=[pl.BlockSpec((tm, d), row), pl.BlockSpec((1, d), const),
                  pl.BlockSpec((d, 2 * w), const), pl.BlockSpec((4 * w, d), const),
                  pl.BlockSpec((16, d), const), pl.BlockSpec((16, 1), const),
                  pl.BlockSpec((w, w), const), pl.BlockSpec((1, w), const),
                  pl.BlockSpec((1, w), const), pl.BlockSpec((HEAD_DIM, LANES), const),
                  pl.BlockSpec((HEAD_DIM, LANES), const),
                  pl.BlockSpec((tm, w), lambda i: (i % tiles, 0)),
                  pl.BlockSpec((tm, w), lambda i: (i % tiles, 0)),
                  pl.BlockSpec((HEAD_DIM // 2, tm), lambda i: (0, i % tiles)),
                  pl.BlockSpec((HEAD_DIM // 2, tm), lambda i: (0, i % tiles))],
        out_specs=[nat, nat, pl.BlockSpec((16, tm), lambda i: (0, i)), tr, tr, tr, tr],
        out_shape=[jax.ShapeDtypeStruct((m, w), F32)] * 2 + [jax.ShapeDtypeStruct((16, m), F32)]
                  + [jax.ShapeDtypeStruct((b, w, t), F32)] * 4,
        compiler_params=_cp(("parallel",)),
        name="attn_proj_prompt",
    )(x, gain, wq, wkv_t, wff, fb, gmat, fqn, mqn, fkn_col, mkn_col, cos, sin, cos_t, sin_t)


def _hgrn_proj_kernel(x_ref, g_ref, w_ref, lb_ref, hq_ref, lf_ref, kin_ref, hi_ref, hg_ref):
    hb = _rmsnorm(x_ref[...], g_ref[...]).astype(BF16)
    z = jnp.dot(hb, w_ref[...], preferred_element_type=F32)
    w = HG_W
    zq = z[:, :w]
    hq_ref[...] = zq * _sigmoid(zq)
    zf = z[:, w:2 * w]
    lb = lb_ref[...]
    lf_ref[...] = jnp.log(lb + (1.0 - lb) * _sigmoid(zf))
    kin_ref[...] = (1.0 - lb) * _sigmoid(-zf)
    hi_ref[...] = z[:, 2 * w:3 * w]
    hg_ref[...] = _sigmoid(z[:, 3 * w:])


def _hgrn_proj(x, gain, w_h, lb, tm):
    m, d = x.shape
    w = HG_W
    row = lambda i: (i, 0)
    const = lambda i: (0, 0)
    return pl.pallas_call(
        _hgrn_proj_kernel,
        grid=(m // tm,),
        in_specs=[pl.BlockSpec((tm, d), row), pl.BlockSpec((1, d), const),
                  pl.BlockSpec((d, 4 * w), const), pl.BlockSpec((1, w), const)],
        out_specs=[pl.BlockSpec((tm, w), row)] * 5,
        out_shape=[jax.ShapeDtypeStruct((m, w), F32)] * 5,
        compiler_params=_cp(("parallel",)),
        name="hgrn_proj",
    )(x, gain, w_h, lb)


def _cumsum_kernel(x_ref, o_ref):
    x = x_ref[...]
    n = x.shape[1]
    lane = _iota(x.shape, 1)
    s = 1
    while s < n:
        x = x + jnp.where(lane >= s, pltpu.roll(x, s, 1), 0.0)
        s *= 2
    o_ref[...] = x


def _cumsum_rows(x, seg):
    r, n = x.shape
    return pl.pallas_call(
        _cumsum_kernel,
        grid=(n // seg,),
        in_specs=[pl.BlockSpec((r, seg), lambda b: (0, b))],
        out_specs=pl.BlockSpec((r, seg), lambda b: (0, b)),
        out_shape=jax.ShapeDtypeStruct((r, n), F32),
        compiler_params=_cp(("parallel",)),
        name="fox_cumsum",
    )(x)


def _chunk_tables(n_q, per_chunk):
    i_tab = np.concatenate([np.full(i // per_chunk + 1, i, np.int32) for i in range(n_q)])
    c_tab = np.concatenate([np.arange(i // per_chunk + 1, dtype=np.int32) for i in range(n_q)])
    return jnp.asarray(i_tab), jnp.asarray(c_tab)


def _store_head_queries(q, qs_ref, scale):
    odd = _iota((1, LANES), 1) >= HEAD_DIM
    for h in range(N_HEADS):
        grp = q[:, (h // 2) * LANES:(h // 2 + 1) * LANES] * scale
        keep = odd if h % 2 else jnp.logical_not(odd)
        qs_ref[h] = jnp.where(keep, grp, 0.0).astype(BF16)


def _flash_update(h, s, visible, vb_t, m_ref, l_ref, acc_ref):
    if visible is not None:
        s = jnp.where(visible, s, NEG)
    m_old = m_ref[h]
    m_new = jnp.maximum(m_old, jnp.max(s, axis=-1, keepdims=True))
    p = jnp.exp(s - m_new[:, :1])
    if visible is not None:
        p = jnp.where(visible, p, 0.0)
    alpha = jnp.exp(m_old - m_new)
    l_ref[h] = alpha * l_ref[h] + jnp.sum(p, axis=-1, keepdims=True)
    pv = lax.dot_general(p.astype(BF16), vb_t, (((1,), (1,)), ((), ())),
                         preferred_element_type=F32)
    acc_ref[h] = alpha * acc_ref[h] + pv
    m_ref[h] = m_new


def _flash_finish(o_ref, l_ref, acc_ref):
    odd = _iota((1, LANES), 1) >= HEAD_DIM
    for g in range(N_HEADS // 2):
        a0 = acc_ref[2 * g] / l_ref[2 * g]
        a1 = acc_ref[2 * g + 1] / l_ref[2 * g + 1]
        o_ref[:, g * LANES:(g + 1) * LANES] = jnp.where(odd, a1, a0)


def _flash_init(q_ref, qs_ref, m_ref, l_ref, acc_ref):
    _store_head_queries(q_ref[...], qs_ref, HEAD_DIM ** -0.5)
    m_ref[...] = jnp.full(m_ref.shape, NEG, F32)
    l_ref[...] = jnp.zeros(l_ref.shape, F32)
    acc_ref[...] = jnp.zeros(acc_ref.shape, F32)


def _fox_kernel(itab, ctab, q_ref, k_ref, v_ref, c_ref, o_ref, qs_ref, m_ref, l_ref, acc_ref,
                *, per_chunk):
    p = pl.program_id(1)
    i = itab[p]
    c = ctab[p]
    tq = q_ref.shape[0]
    tk = k_ref.shape[1]

    @pl.when(c == 0)
    def _():
        _flash_init(q_ref, qs_ref, m_ref, l_ref, acc_ref)

    def step(visible):
        for g in range(N_HEADS // 2):
            rows = slice(g * LANES, (g + 1) * LANES)
            kb = k_ref[rows, :].astype(BF16)
            vb = v_ref[rows, :].astype(BF16)
            for h in (2 * g, 2 * g + 1):
                s = jnp.dot(qs_ref[h], kb, preferred_element_type=F32) - c_ref[h:h + 1, :]
                _flash_update(h, s, visible, vb, m_ref, l_ref, acc_ref)

    last = i // per_chunk

    @pl.when(c < last)
    def _():
        step(None)

    @pl.when(c == last)
    def _():
        step(c * tk + _iota((tq, tk), 1) <= i * tq + _iota((tq, tk), 0))
        _flash_finish(o_ref, l_ref, acc_ref)


def _fox_prompt(q, k_t, v_t, c, b, t, tq, tk):
    nq = t // tq
    nc = t // tk
    per_chunk = tk // tq
    itab, ctab = _chunk_tables(nq, per_chunk)
    w = ATT_W
    qmap = lambda bb, p, it, ct: (bb * nq + it[p], 0)
    kmap = lambda bb, p, it, ct: (bb, 0, ct[p])
    grid_spec = pltpu.PrefetchScalarGridSpec(
        num_scalar_prefetch=2,
        grid=(b, int(itab.shape[0])),
        in_specs=[pl.BlockSpec((tq, w), qmap),
                  pl.BlockSpec((None, w, tk), kmap),
                  pl.BlockSpec((None, w, tk), kmap),
                  pl.BlockSpec((16, tk), lambda bb, p, it, ct: (0, bb * nc + ct[p]))],
        out_specs=pl.BlockSpec((tq, w), qmap),
        scratch_shapes=[pltpu.VMEM((N_HEADS, tq, LANES), BF16),
                        pltpu.VMEM((N_HEADS, tq, LANES), F32),
                        pltpu.VMEM((N_HEADS, tq, LANES), F32),
                        pltpu.VMEM((N_HEADS, tq, LANES), F32)])
    return pl.pallas_call(
        functools.partial(_fox_kernel, per_chunk=per_chunk), grid_spec=grid_spec,
        out_shape=jax.ShapeDtypeStruct((b * t, w), F32),
        compiler_params=_cp(("parallel", "arbitrary")),
        name="fox_prompt",
    )(itab, ctab, q, k_t, v_t, c)


def _block_mean_kernel(k_ref, o_ref):
    nb = o_ref.shape[1]
    lane = _iota(o_ref.shape, 1)
    out = jnp.zeros(o_ref.shape, F32)
    for n in range(nb):
        col = jnp.mean(k_ref[:, n * MOBA_BLOCK:(n + 1) * MOBA_BLOCK], axis=-1, keepdims=True)
        out = jnp.where(lane == n, col, out)
    o_ref[...] = out


def _block_means(k_t, nb):
    b, w, t = k_t.shape
    return pl.pallas_call(
        _block_mean_kernel,
        grid=(b,),
        in_specs=[pl.BlockSpec((None, w, t), lambda bb: (bb, 0, 0))],
        out_specs=pl.BlockSpec((None, w, nb), lambda bb: (bb, 0, 0)),
        out_shape=jax.ShapeDtypeStruct((b, w, nb), F32),
        compiler_params=_cp(("parallel",)),
        name="moba_block_means",
    )(k_t)


def _top_blocks(g, n_valid):
    nb = g.shape[1]
    col = _iota(g.shape, 1)
    g = jnp.where(col < n_valid, g, NEG)
    sel = jnp.zeros(g.shape, F32)
    for _ in range(MOBA_TOPK):
        mx = jnp.max(g, axis=-1, keepdims=True)
        idx = jnp.min(jnp.where(g == mx, col, nb), axis=-1, keepdims=True)
        hit = col == idx
        sel = jnp.where(jnp.logical_and(hit, mx > 0.5 * NEG), 1.0, sel)
        g = jnp.where(hit, REMOVED, g)
    return sel


def _moba_kernel(itab, ctab, q_ref, k_ref, v_ref, km_ref, o_ref,
                 qs_ref, sel_ref, m_ref, l_ref, acc_ref, *, per_chunk):
    p = pl.program_id(1)
    i = itab[p]
    c = ctab[p]
    tq = q_ref.shape[0]

    @pl.when(c == 0)
    def _():
        q = q_ref[...]
        _flash_init(q_ref, qs_ref, m_ref, l_ref, acc_ref)
        odd = _iota((1, LANES), 1) >= HEAD_DIM
        for h in range(N_HEADS):
            lanes = slice((h // 2) * LANES, (h // 2 + 1) * LANES)
            keep = odd if h % 2 else jnp.logical_not(odd)
            qh, ql = _pieces(jnp.where(keep, q[:, lanes], 0.0), 2)
            kh, kl = _pieces(km_ref[lanes, :], 2)
            g = (jnp.dot(qh, kh, preferred_element_type=F32)
                 + jnp.dot(qh, kl, preferred_element_type=F32)
                 + jnp.dot(ql, kh, preferred_element_type=F32))
            sel_ref[h] = _top_blocks(g, i)

    def step(has_own_block):
        causal = (_iota((tq, tq), 1) <= _iota((tq, tq), 0)).astype(F32)
        for g in range(N_HEADS // 2):
            rows = slice(g * LANES, (g + 1) * LANES)
            kb = k_ref[rows, :].astype(BF16)
            vb = v_ref[rows, :].astype(BF16)
            for h in (2 * g, 2 * g + 1):
                s = jnp.dot(qs_ref[h], kb, preferred_element_type=F32)
                sel = sel_ref[h]
                col = _iota(sel.shape, 1)
                vis = []
                for n in range(per_chunk):
                    blk = c * per_chunk + n
                    picked = jnp.sum(jnp.where(col == blk, sel, 0.0), axis=-1, keepdims=True)
                    if has_own_block:
                        picked = picked + (blk == i).astype(F32) * causal
                    vis.append(jnp.broadcast_to(picked > 0.5, (tq, tq)))
                _flash_update(h, s, jnp.concatenate(vis, axis=1), vb, m_ref, l_ref, acc_ref)

    last = i // per_chunk

    @pl.when(c < last)
    def _():
        step(False)

    @pl.when(c == last)
    def _():
        step(True)
        _flash_finish(o_ref, l_ref, acc_ref)


def _moba_prompt(q, k_t, v_t, km_t, b, t, tk):
    tq = MOBA_BLOCK
    nb = t // tq
    per_chunk = tk // tq
    itab, ctab = _chunk_tables(nb, per_chunk)
    w = ATT_W
    qmap = lambda bb, p, it, ct: (bb * nb + it[p], 0)
    kmap = lambda bb, p, it, ct: (bb, 0, ct[p])
    grid_spec = pltpu.PrefetchScalarGridSpec(
        num_scalar_prefetch=2,
        grid=(b, int(itab.shape[0])),
        in_specs=[pl.BlockSpec((tq, w), qmap),
                  pl.BlockSpec((None, w, tk), kmap),
                  pl.BlockSpec((None, w, tk), kmap),
                  pl.BlockSpec((None, w, nb), lambda bb, p, it, ct: (bb, 0, 0))],
        out_specs=pl.BlockSpec((tq, w), qmap),
        scratch_shapes=[pltpu.VMEM((N_HEADS, tq, LANES), BF16),
                        pltpu.VMEM((N_HEADS, tq, nb), F32),
                        pltpu.VMEM((N_HEADS, tq, LANES), F32),
                        pltpu.VMEM((N_HEADS, tq, LANES), F32),
                        pltpu.VMEM((N_HEADS, tq, LANES), F32)])
    return pl.pallas_call(
        functools.partial(_moba_kernel, per_chunk=per_chunk), grid_spec=grid_spec,
        out_shape=jax.ShapeDtypeStruct((b * t, w), F32),
        compiler_params=_cp(("parallel", "arbitrary")),
        name="moba_prompt",
    )(itab, ctab, q, k_t, v_t, km_t)


def _hgrn_kernel(q_ref, f_ref, k_ref, v_ref, g_ref, on_ref, o_ref, s_ref, st_ref, *, n_chunks):
    tstep = pl.program_id(2)
    C, c = HG_CHUNK, HG_SUB

    @pl.when(tstep == 0)
    def _():
        st_ref[...] = jnp.zeros(st_ref.shape, F32)

    ltri = (_iota((C, C), 0) >= _iota((C, C), 1)).astype(BF16)
    trow = _iota((c, HG_D), 0)

    def chunk(ci, carry):
        r0 = pl.multiple_of(ci * C, C)
        q = q_ref[pl.ds(r0, C), :]
        lf = f_ref[pl.ds(r0, C), :]
        kin = k_ref[pl.ds(r0, C), :]
        v = v_ref[pl.ds(r0, C), :]
        b = sum(jnp.dot(ltri, part, preferred_element_type=F32) for part in _pieces(lf, 3))
        st = st_ref[...]
        o_inter = _dot_nt(q * jnp.exp(b), st)
        b_last = b[C - 1:C, :]
        st_ref[...] = st * jnp.exp(b_last) + _dot_tn(v, kin * jnp.exp(b_last - b))

        out = [o_inter[n * c:(n + 1) * c] for n in range(C // c)]
        for jj in range(C // c):
            rows = slice(jj * c, (jj + 1) * c)
            qj, kj, bj, vj = q[rows], kin[rows], b[rows], v[rows]
            acc = jnp.zeros((c, HG_D), F32)
            for s in range(c):
                decay = jnp.exp(jnp.where(trow >= s, bj - bj[s:s + 1], NEG))
                a = jnp.sum(qj * kj[s:s + 1] * decay, axis=-1, keepdims=True)
                acc = acc + a * vj[s:s + 1]
            out[jj] = out[jj] + acc
            if jj + 1 < C // c:
                b_end = bj[c - 1:c]
                kp = kj * jnp.exp(b_end - bj)
                qp = q[(jj + 1) * c:] * jnp.exp(b[(jj + 1) * c:] - b_end)
                oo = _dot(_dot_nt(qp, kp), vj)
                for n in range(jj + 1, C // c):
                    out[n] = out[n] + oo[(n - jj - 1) * c:(n - jj) * c]
        o = jnp.concatenate(out, axis=0)
        o_ref[pl.ds(r0, C), :] = _rmsnorm(o, on_ref[...]) * g_ref[pl.ds(r0, C), :]
        return carry

    lax.fori_loop(0, n_chunks, chunk, 0)

    @pl.when(tstep == pl.num_programs(2) - 1)
    def _():
        s_ref[...] = st_ref[...].T


def _hgrn_prompt(q, lf, kin, v, gate, onorm, b, t, tb):
    nt = t // tb
    blk = pl.BlockSpec((tb, HG_D), lambda bb, h, tt: (bb * nt + tt, h))
    return pl.pallas_call(
        functools.partial(_hgrn_kernel, n_chunks=tb // HG_CHUNK),
        grid=(b, HG_HEADS, nt),
        in_specs=[blk] * 5 + [pl.BlockSpec((1, HG_D), lambda bb, h, tt: (0, 0))],
        out_specs=[blk, pl.BlockSpec((None, None, HG_D, HG_D), lambda bb, h, tt: (bb, h, 0, 0))],
        out_shape=[jax.ShapeDtypeStruct((b * t, HG_W), F32),
                   jax.ShapeDtypeStruct((b, HG_HEADS, HG_D, HG_D), F32)],
        scratch_shapes=[pltpu.VMEM((HG_D, HG_D), F32)],
        compiler_params=_cp(("parallel", "parallel", "arbitrary")),
        name="hgrn_prompt",
    )(q, lf, kin, v, gate, onorm)


def _merge_kernel(x_ref, g_ref, wg_ref, of_ref, oh_ref, om_ref, wf_ref, wh_ref, wm_ref,
                  wo_ref, y_ref):
    x = x_ref[...]
    d = x.shape[1]
    hb = _rmsnorm(x, g_ref[...]).astype(BF16)
    gates = _sigmoid(jnp.dot(hb, wg_ref[...], preferred_element_type=F32))
    y = (gates[:, :d] * _dot(of_ref[...], wf_ref[...])
         + gates[:, d:2 * d] * _dot(oh_ref[...], wh_ref[...])
         + gates[:, 2 * d:] * _dot(om_ref[...], wm_ref[...]))
    y_ref[...] = x + _dot(y, wo_ref[...])


def _merge(x, gain, wg, o_f, o_h, o_m, w_bf, w_bh, w_bm, w_out, tm):
    m, d = x.shape
    w = ATT_W
    row = lambda i: (i, 0)
    const = lambda i: (0, 0)
    return pl.pallas_call(
        _merge_kernel,
        grid=(m // tm,),
        in_specs=[pl.BlockSpec((tm, d), row), pl.BlockSpec((1, d), const),
                  pl.BlockSpec((d, N_BRANCH * d), const)]
                 + [pl.BlockSpec((tm, w), row)] * 3 + [pl.BlockSpec((w, d), const)] * 3
                 + [pl.BlockSpec((d, d), const)],
        out_specs=pl.BlockSpec((tm, d), row),
        out_shape=jax.ShapeDtypeStruct((m, d), F32),
        compiler_params=_cp(("parallel",)),
        name="branch_merge",
    )(x, gain, wg, o_f, o_h, o_m, w_bf, w_bh, w_bm, w_out)


def _ffn_kernel(x_ref, g_ref, wa_ref, wg_ref, cwa_ref, cwg_ref, cba_ref, cbg_ref, wd_ref,
                y_ref, ca_ref, cg_ref, hb_ref, acc_ref, carry_ref, *, tiles_per_seq):
    i = pl.program_id(0)
    f = pl.program_id(1)
    tm = x_ref.shape[0]

    @pl.when(f == 0)
    def _():
        hb_ref[...] = _rmsnorm(x_ref[...], g_ref[...]).astype(BF16)
        acc_ref[...] = jnp.zeros(acc_ref.shape, F32)

    hb = hb_ref[...]
    rowi = _iota((tm, wa_ref.shape[1]), 0)

    @pl.when((i % tiles_per_seq) == 0)
    def _():
        carry_ref[f] = jnp.zeros(carry_ref.shape[1:], F32)

    def conv(u, cw_ref, cb_ref, slot, tail_ref):
        prev = carry_ref[f, slot]
        p1 = prev[SUBLANES - 1:SUBLANES]
        p2 = prev[SUBLANES - 2:SUBLANES - 1]
        u1 = jnp.where(rowi == 0, p1, pltpu.roll(u, 1, 0))
        u2 = jnp.where(rowi == 0, p2, jnp.where(rowi == 1, p1, pltpu.roll(u, 2, 0)))
        cw = cw_ref[...]
        tail = u[tm - SUBLANES:]
        carry_ref[f, slot] = tail
        tail_ref[...] = tail
        return cb_ref[...] + cw[0:1] * u2 + cw[1:2] * u1 + cw[2:3] * u

    a = conv(jnp.dot(hb, wa_ref[...], preferred_element_type=F32), cwa_ref, cba_ref, 0, ca_ref)
    g = conv(jnp.dot(hb, wg_ref[...], preferred_element_type=F32), cwg_ref, cbg_ref, 1, cg_ref)
    acc_ref[...] += _dot(a * _sigmoid(a) * g, wd_ref[...])

    @pl.when(f == pl.num_programs(1) - 1)
    def _():
        y_ref[...] = x_ref[...] + acc_ref[...]


def _ffn_prompt(x, gain, w_up, conv_w, conv_b, w_down, b, t, tm, fc):
    m, d = x.shape
    dff = w_down.shape[0]
    nf = dff // fc
    tiles = t // tm
    return pl.pallas_call(
        functools.partial(_ffn_kernel, tiles_per_seq=tiles),
        grid=(m // tm, nf),
        in_specs=[pl.BlockSpec((tm, d), lambda i, f: (i, 0)),
                  pl.BlockSpec((1, d), lambda i, f: (0, 0)),
                  pl.BlockSpec((d, fc), lambda i, f: (0, f)),
                  pl.BlockSpec((d, fc), lambda i, f: (0, nf + f)),
                  pl.BlockSpec((CONV_W, fc), lambda i, f: (0, f)),
                  pl.BlockSpec((CONV_W, fc), lambda i, f: (0, nf + f)),
                  pl.BlockSpec((1, fc), lambda i, f: (0, f)),
                  pl.BlockSpec((1, fc), lambda i, f: (0, nf + f)),
                  pl.BlockSpec((fc, d), lambda i, f: (f, 0))],
        out_specs=[pl.BlockSpec((tm, d), lambda i, f: (i, 0)),
                   pl.BlockSpec((None, SUBLANES, fc), lambda i, f: (i, 0, f)),
                   pl.BlockSpec((None, SUBLANES, fc), lambda i, f: (i, 0, f))],
        out_shape=[jax.ShapeDtypeStruct((m, d), F32),
                   jax.ShapeDtypeStruct((m // tm, SUBLANES, dff), F32),
                   jax.ShapeDtypeStruct((m // tm, SUBLANES, dff), F32)],
        scratch_shapes=[pltpu.VMEM((tm, d), BF16), pltpu.VMEM((tm, d), F32),
                        pltpu.VMEM((nf, 2, SUBLANES, fc), F32)],
        compiler_params=_cp(("arbitrary", "arbitrary")),
        name="conv_ffn_prompt",
    )(x, gain, w_up, w_up, conv_w, conv_w, conv_b, conv_b, w_down)


def _ffn_step_kernel(x_ref, g_ref, wa_ref, wg_ref, cwa_ref, cwg_ref, cba_ref, cbg_ref,
                     pa0_ref, pa1_ref, pg0_ref, pg1_ref, wd_ref, y_ref, ua_ref, ug_ref,
                     hb_ref, acc_ref):
    f = pl.program_id(0)

    @pl.when(f == 0)
    def _():
        hb_ref[...] = _rmsnorm(x_ref[...], g_ref[...]).astype(BF16)
        acc_ref[...] = jnp.zeros(acc_ref.shape, F32)

    hb = hb_ref[...]

    def conv(w_ref, cw_ref, cb_ref, p0_ref, p1_ref, u_ref):
        u = jnp.dot(hb, w_ref[...], preferred_element_type=F32)
        u_ref[...] = u
        cw = cw_ref[...]
        return cb_ref[...] + cw[0:1] * p0_ref[...] + cw[1:2] * p1_ref[...] + cw[2:3] * u

    a = conv(wa_ref, cwa_ref, cba_ref, pa0_ref, pa1_ref, ua_ref)
    g = conv(wg_ref, cwg_ref, cbg_ref, pg0_ref, pg1_ref, ug_ref)
    acc_ref[...] += _dot(a * _sigmoid(a) * g, wd_ref[...])

    @pl.when(f == pl.num_programs(0) - 1)
    def _():
        y_ref[...] = x_ref[...] + acc_ref[...]


def _ffn_step(x, gain, w_up, conv_w, conv_b, w_down, prev0, prev1, fc):
    m, d = x.shape
    dff = w_down.shape[0]
    nf = dff // fc
    lo = lambda f: (0, f)
    hi = lambda f: (0, nf + f)
    return pl.pallas_call(
        _ffn_step_kernel,
        grid=(nf,),
        in_specs=[pl.BlockSpec((m, d), lambda f: (0, 0)), pl.BlockSpec((1, d), lambda f: (0, 0)),
                  pl.BlockSpec((d, fc), lo), pl.BlockSpec((d, fc), hi),
                  pl.BlockSpec((CONV_W, fc), lo), pl.BlockSpec((CONV_W, fc), hi),
                  pl.BlockSpec((1, fc), lo), pl.BlockSpec((1, fc), hi),
                  pl.BlockSpec((m, fc), lo), pl.BlockSpec((m, fc), lo),
                  pl.BlockSpec((m, fc), hi), pl.BlockSpec((m, fc), hi),
                  pl.BlockSpec((fc, d), lambda f: (f, 0))],
        out_specs=[pl.BlockSpec((m, d), lambda f: (0, 0)),
                   pl.BlockSpec((m, fc), lo), pl.BlockSpec((m, fc), lo)],
        out_shape=[jax.ShapeDtypeStruct((m, d), F32), jax.ShapeDtypeStruct((m, dff), F32),
                   jax.ShapeDtypeStruct((m, dff), F32)],
        scratch_shapes=[pltpu.VMEM((m, d), BF16), pltpu.VMEM((m, d), F32)],
        compiler_params=_cp(("arbitrary",)),
        name="conv_ffn_step",
    )(x, gain, w_up, w_up, conv_w, conv_w, conv_b, conv_b, prev0, prev1, prev0, prev1, w_down)


def _decode_scan_kernel(pt_ref, fq_ref, mq_ref, lfn_ref, kn_ref, vn_ref, *rest,
                        pages_per_step, n_blocks):
    pp = pages_per_step
    fk = rest[0:pp]
    fv = rest[pp:2 * pp]
    lf = rest[2 * pp:3 * pp]
    mk = rest[3 * pp:4 * pp]
    o_ref, top_ref, m_ref, l_ref, acc_ref, carry_ref, gate_ref = rest[4 * pp:]
    g = pl.program_id(1)
    d = HEAD_DIM

    @pl.when(g == 0)
    def _():
        m_ref[...] = jnp.full(m_ref.shape, NEG, F32)
        l_ref[...] = jnp.zeros(l_ref.shape, F32)
        acc_ref[...] = jnp.zeros(acc_ref.shape, F32)
        carry_ref[...] = jnp.zeros(carry_ref.shape, F32)
        gate_ref[...] = jnp.zeros(gate_ref.shape, F32)

    lane = _iota((1, PAGE), 1)
    utri = (_iota((PAGE, PAGE), 0) <= _iota((PAGE, PAGE), 1)).astype(BF16)
    zeros8 = jnp.zeros((SUBLANES, PAGE), F32)

    def attend(h, k_t, v_t, c_row, valid):
        q_col = fq_ref[h * d:(h + 1) * d, :] * (HEAD_DIM ** -0.5)
        z = jnp.sum(k_t * q_col, axis=0, keepdims=True) - c_row
        if valid is not None:
            z = jnp.where(valid, z, NEG)
        m_old = m_ref[h:h + 1, :]
        m_new = jnp.maximum(m_old, jnp.max(z, axis=-1, keepdims=True))
        p = jnp.exp(z - m_new)
        if valid is not None:
            p = jnp.where(valid, p, 0.0)
        alpha = jnp.exp(m_old - m_new)
        l_ref[h:h + 1, :] = alpha * l_ref[h:h + 1, :] + jnp.sum(p, axis=-1, keepdims=True)
        acc_ref[h] = alpha * acc_ref[h] + p * v_t
        m_ref[h:h + 1, :] = m_new

    for r in range(pp):
        parts = [part.astype(F32) for part in _pieces(lf[r][...], 3)]
        stacked = jnp.concatenate(parts + [zeros8], axis=0).astype(BF16)
        y = jnp.dot(stacked, utri, preferred_element_type=F32)
        c_t = carry_ref[...] + y[0:8] + y[8:16] + y[16:24]
        carry_ref[...] = jnp.broadcast_to(c_t[:, PAGE - 1:PAGE], c_t.shape)
        blk = (g * pp + r) // 2
        for h in range(N_HEADS):
            attend(h, fk[r][h], fv[r][h], c_t[h:h + 1, :], None)
            sm = jnp.sum(mk[r][h] * mq_ref[h * d:(h + 1) * d, :], axis=0, keepdims=True)
            tot = jnp.sum(sm, axis=-1, keepdims=True)
            gate_ref[h:h + 1, :] += jnp.where(lane == blk, tot, 0.0)

    @pl.when(g == pl.num_programs(1) - 1)
    def _():
        c_new = carry_ref[...] + lfn_ref[...]
        for h in range(N_HEADS):
            attend(h, kn_ref[h * d:(h + 1) * d, :], vn_ref[h * d:(h + 1) * d, :],
                   c_new[h:h + 1, :], lane == 0)
            tot = jnp.sum(acc_ref[h], axis=-1, keepdims=True)
            o_ref[h * d:(h + 1) * d, :] = jnp.broadcast_to(tot, (d, PAGE)) / l_ref[h:h + 1, :]

        lanes8 = _iota((SUBLANES, PAGE), 1)
        gate = jnp.where(lanes8 < n_blocks, gate_ref[...] * (1.0 / MOBA_BLOCK), REMOVED)
        top = jnp.zeros((SUBLANES, PAGE), jnp.int32)
        for r in range(MOBA_TOPK):
            mx = jnp.max(gate, axis=-1, keepdims=True)
            idx = jnp.min(jnp.where(gate == mx, lanes8, PAGE), axis=-1, keepdims=True)
            top = jnp.where(lanes8 == r, idx, top)
            gate = jnp.where(lanes8 == idx, REMOVED, gate)
        top_ref[...] = top


def _decode_scan(page_table, layer, fq, mq, lf_new, k_new, v_new,
                 cache_fk, cache_fv, cache_lf, cache_mk):
    db, n_pages = page_table.shape
    pp = DEC_PAGES_PER_STEP
    n_blocks = n_pages * PAGE // MOBA_BLOCK
    w = ATT_W
    seq = lambda rows: pl.BlockSpec((None, rows, PAGE), lambda b, g, pt: (b, 0, 0))

    def page(r):
        return pl.BlockSpec((None, None, N_HEADS, HEAD_DIM, PAGE),
                            lambda b, g, pt, r=r: (layer, pt[b, g * pp + r], 0, 0, 0))

    def gate_page(r):
        return pl.BlockSpec((None, None, N_HEADS, PAGE),
                            lambda b, g, pt, r=r: (layer, pt[b, g * pp + r], 0, 0))

    in_specs = [seq(w), seq(w), seq(N_HEADS), seq(w), seq(w)]
    in_specs += [page(r) for r in range(pp)] * 2 + [gate_page(r) for r in range(pp)]
    in_specs += [page(r) for r in range(pp)]
    grid_spec = pltpu.PrefetchScalarGridSpec(
        num_scalar_prefetch=1, grid=(db, n_pages // pp), in_specs=in_specs,
        out_specs=[seq(w), seq(N_HEADS)],
        scratch_shapes=[pltpu.VMEM((N_HEADS, PAGE), F32), pltpu.VMEM((N_HEADS, PAGE), F32),
                        pltpu.VMEM((N_HEADS, HEAD_DIM, PAGE), F32),
                        pltpu.VMEM((N_HEADS, PAGE), F32), pltpu.VMEM((N_HEADS, PAGE), F32)])
    return pl.pallas_call(
        functools.partial(_decode_scan_kernel, pages_per_step=pp, n_blocks=n_blocks),
        grid_spec=grid_spec,
        out_shape=[jax.ShapeDtypeStruct((db, w, PAGE), F32),
                   jax.ShapeDtypeStruct((db, N_HEADS, PAGE), jnp.int32)],
        compiler_params=_cp(("parallel", "arbitrary")),
        name="decode_scan",
    )(page_table, fq, mq, lf_new, k_new, v_new,
      *([cache_fk] * pp), *([cache_fv] * pp), *([cache_lf] * pp), *([cache_mk] * pp))


def _moba_step_kernel(pt_ref, top_ref, q_ref, kn_ref, vn_ref, *rest):
    n = 2 * MOBA_TOPK
    kp = rest[0:n]
    vp = rest[n:2 * n]
    o_ref = rest[2 * n]
    q8 = jnp.broadcast_to(q_ref[...] * (HEAD_DIM ** -0.5), (SUBLANES, HEAD_DIM))
    s_own = jnp.sum(q8 * kn_ref[...], axis=-1, keepdims=True)
    scores = [_dot(q8, kp[r][...]) for r in range(n)]
    m = s_own
    for s in scores:
        m = jnp.maximum(m, jnp.max(s, axis=-1, keepdims=True))
    p_own = jnp.exp(s_own - m)
    l = p_own
    acc = p_own * vn_ref[...]
    for r, s in enumerate(scores):
        p = jnp.exp(s - m)
        l = l + jnp.sum(p, axis=-1, keepdims=True)
        acc = acc + _dot_nt(p, vp[r][...])
    o_ref[...] = (acc / l)[0:1]


def _moba_step(page_table, top_flat, layer, mq, k_new, v_new, cache_mk, cache_mv):
    db = page_table.shape[0]
    vec = pl.BlockSpec((None, None, 1, HEAD_DIM), lambda b, h, pt, tp: (b, h, 0, 0))

    def page(r, half):
        def index(b, h, pt, tp):
            blk = tp[(b * N_HEADS + h) * MOBA_TOPK + r]
            return (layer, pt[b, 2 * blk + half], h, 0, 0)
        return pl.BlockSpec((None, None, None, HEAD_DIM, PAGE), index)

    pages = [page(r, half) for r in range(MOBA_TOPK) for half in range(2)]
    grid_spec = pltpu.PrefetchScalarGridSpec(
        num_scalar_prefetch=2, grid=(db, N_HEADS),
        in_specs=[vec, vec, vec] + pages + pages,
        out_specs=vec)
    n = len(pages)
    return pl.pallas_call(
        _moba_step_kernel, grid_spec=grid_spec,
        out_shape=jax.ShapeDtypeStruct((db, N_HEADS, 1, HEAD_DIM), F32),
        compiler_params=_cp(("parallel", "parallel")),
        name="moba_step",
    )(page_table, top_flat, mq, k_new, v_new, *([cache_mk] * n), *([cache_mv] * n))


def _hgrn_step_kernel(q_ref, f_ref, k_ref, v_ref, g_ref, on_ref, s0_ref, o_ref, s_ref):
    def column(row):
        return jnp.broadcast_to(row, (HG_D, HG_D)).T

    s_new = column(jnp.exp(f_ref[...])) * s0_ref[...] + column(k_ref[...]) * v_ref[...]
    s_ref[...] = s_new
    o = jnp.sum(column(q_ref[...]) * s_new, axis=0, keepdims=True)
    o_ref[...] = _rmsnorm(o, on_ref[...]) * g_ref[...]


def _hgrn_step(q, lf, kin, v, gate, onorm, s0):
    db = q.shape[0]
    vec = pl.BlockSpec((None, 1, HG_D), lambda b, h: (b, 0, h))
    mat = pl.BlockSpec((None, None, HG_D, HG_D), lambda b, h: (b, h, 0, 0))
    return pl.pallas_call(
        _hgrn_step_kernel,
        grid=(db, HG_HEADS),
        in_specs=[vec] * 5 + [pl.BlockSpec((1, HG_D), lambda b, h: (0, 0)), mat],
        out_specs=[vec, mat],
        out_shape=[jax.ShapeDtypeStruct((db, 1, HG_W), F32), jax.ShapeDtypeStruct(s0.shape, F32)],
        compiler_params=_cp(("parallel", "parallel")),
        name="hgrn_step",
    )(q, lf, kin, v, gate, onorm, s0)


def _rope_tables(pos):
    half = HEAD_DIM // 2
    inv = ROPE_THETA ** (-jnp.arange(half, dtype=F32) / half)
    ang = pos.astype(F32)[:, None] * inv[None, :]
    cos = jnp.tile(jnp.cos(ang), (1, 2 * N_HEADS))
    sin = jnp.sin(ang)
    sin = jnp.tile(jnp.concatenate([-sin, sin], axis=1), (1, N_HEADS))
    return cos, sin


def _row_tile(m, want):
    return want if m % want == 0 else m


def kernel(x_prompt, x_sample, page_table, cache_fox_k, cache_fox_v, cache_fox_logf,
           cache_moba_k, cache_moba_v, state_hgrn, state_ffn_conv,
           norm_mix, w_in, fox_fb, fox_qnorm, fox_knorm, moba_qnorm, moba_knorm,
           hgrn_lb_logits, hgrn_onorm, w_branch_fox, w_branch_hgrn, w_branch_moba,
           w_out, norm_ffn, ffn_up, ffn_conv_w, ffn_conv_b, ffn_down):
    depth = w_in.shape[0]
    bp, tp, d = x_prompt.shape
    db, ts, _ = x_sample.shape
    n_pages = page_table.shape[1]
    past_len = n_pages * PAGE
    dff = ffn_down.shape[1]
    w = ATT_W

    sm = jax.nn.softmax(hgrn_lb_logits.astype(F32), axis=0)
    lower = jnp.clip(jnp.cumsum(sm, axis=0) - sm[0:1], 0.0, 0.999)

    head_of = np.arange(w) // HEAD_DIM
    gmat = jnp.asarray((head_of[:, None] == head_of[None, :]) / HEAD_DIM, BF16)

    cos_p, sin_p = _rope_tables(jnp.arange(tp))
    cos_s, sin_s = _rope_tables(jnp.full((db * ts,), past_len) + jnp.tile(jnp.arange(ts), db))
    half = HEAD_DIM // 2
    cos_pt, sin_pt = cos_p[:, :half].T, sin_p[:, half:HEAD_DIM].T

    to_pages = lambda a: jnp.transpose(a, (0, 1, 3, 4, 2))
    c_fk, c_fv, c_mk, c_mv = (to_pages(a) for a in (cache_fox_k, cache_fox_v,
                                                     cache_moba_k, cache_moba_v))
    c_lf = jnp.transpose(cache_fox_logf, (0, 1, 3, 2))

    xp = x_prompt.reshape(bp * tp, d)
    xs = x_sample.reshape(db * ts, d)
    mp, ms = xp.shape[0], xs.shape[0]
    new_p, new_s = [], []
    tile = lambda v: jnp.tile(v, N_HEADS)[None, :]
    wide = lambda v: jnp.broadcast_to(v[:, None], (HEAD_DIM, LANES))
    w_in_t = jnp.swapaxes(w_in, 1, 2)

    for l in range(depth):
        wl, wt = w_in[l], w_in_t[l]
        o_fq, o_fk, o_fv, o_ff = 0, w, 2 * w, 3 * w
        o_hg = o_ff + N_HEADS
        o_mq = o_hg + 4 * HG_W
        o_mk, o_mv, o_gate = o_mq + w, o_mq + 2 * w, o_mq + 3 * w
        wfox = wl[:, o_fq:o_ff].astype(BF16)
        wmoba = wl[:, o_mq:o_gate].astype(BF16)
        wq = jnp.concatenate([wl[:, o_fq:o_fk], wl[:, o_mq:o_mk]], axis=1).astype(BF16)
        wkv_t = jnp.concatenate([wt[o_fk:o_ff], wt[o_mk:o_gate]], axis=0).astype(BF16)
        wff = jnp.zeros((16, d), BF16).at[:N_HEADS].set(wt[o_ff:o_hg].astype(BF16))
        whg = wl[:, o_hg:o_mq].astype(BF16)
        wgate = wl[:, o_gate:].astype(BF16)
        fb = jnp.zeros((16, 1), F32).at[:N_HEADS, 0].set(fox_fb[l])
        gmix = norm_mix[l][None, :]
        lb = lower[l][None, :]
        onorm = hgrn_onorm[l][None, :]
        norms = (tile(fox_qnorm[l]), tile(fox_knorm[l]), tile(moba_qnorm[l]), tile(moba_knorm[l]))
        w_bf, w_bh, w_bm = (a[l].astype(BF16) for a in (w_branch_fox, w_branch_hgrn, w_branch_moba))
        w_o = w_out[l].astype(BF16)
        gffn = norm_ffn[l][None, :]
        up = ffn_up[l].astype(BF16)
        down = ffn_down[l].astype(BF16)
        cw = ffn_conv_w[l]
        cb = ffn_conv_b[l][None, :]

        tm = _row_tile(tp, 512)
        fq, mq, lft, fk_t, fv_t, mk_t, mv_t = _attn_proj_t(
            xp, gmix, wq, wkv_t, wff, fb, gmat, norms[0], norms[2], wide(fox_knorm[l]),
            wide(moba_knorm[l]), cos_p, sin_p, cos_pt, sin_pt, bp, tp, tm)
        hq, hlf, hkin, hi, hg = _hgrn_proj(xp, gmix, whg, lb, tm)
        c = _cumsum_rows(lft, tp)
        tk = _row_tile(tp, 4 * MOBA_BLOCK)
        o_f = _fox_prompt(fq, fk_t, fv_t, c, bp, tp, MOBA_BLOCK, tk)
        km_t = _block_means(mk_t, tp // MOBA_BLOCK)
        o_m = _moba_prompt(mq, mk_t, mv_t, km_t, bp, tp, tk)
        o_h, s_p = _hgrn_prompt(hq, hlf, hkin, hi, hg, onorm, bp, tp, _row_tile(tp, 512))
        xp = _merge(xp, gmix, wgate, o_f, o_h, o_m, w_bf, w_bh, w_bm, w_o, tm)
        xp, ta, tg = _ffn_prompt(xp, gffn, up, cw, cb, down, bp, tp, _row_tile(tp, 1024), 256)
        tiles = tp // _row_tile(tp, 1024)
        conv_p = jnp.concatenate([ta, tg], axis=-1)[tiles - 1::tiles, SUBLANES - (CONV_W - 1):]
        heads_last = lambda a: a.reshape(bp, N_HEADS, HEAD_DIM, tp).transpose(0, 3, 1, 2)
        new_p.append((heads_last(fk_t), heads_last(fv_t), lft[:N_HEADS].T.reshape(bp, tp, N_HEADS),
                      heads_last(mk_t), heads_last(mv_t), s_p, conv_p))

        fq, fk, fv, lft, mq, mk, mv = _attn_proj(xs, gmix, wfox, wff, fb, wmoba, gmat, *norms,
                                                 cos_s, sin_s, ms)
        hq, hlf, hkin, hi, hg = _hgrn_proj(xs, gmix, whg, lb, ms)
        r3 = lambda a: a.reshape(db, 1, a.shape[-1])
        hd1 = lambda a: a.reshape(db, N_HEADS, 1, HEAD_DIM)
        col = lambda a: jnp.broadcast_to(a[:, :, None], (db, w, PAGE))
        lf_new = jnp.broadcast_to(lft[:N_HEADS].T[:, :, None], (db, N_HEADS, PAGE))
        o_f, top = _decode_scan(page_table, l, col(fq), col(mq), lf_new, col(fk), col(fv),
                                c_fk, c_fv, c_lf, c_mk)
        o_f = o_f[:, :, 0]
        top_flat = top[:, :, :MOBA_TOPK].reshape(-1)
        o_m = _moba_step(page_table, top_flat, l, hd1(mq), hd1(mk), hd1(mv), c_mk, c_mv)
        o_h, s_s = _hgrn_step(r3(hq), r3(hlf), r3(hkin), r3(hi), r3(hg), onorm, state_hgrn[l])
        xs = _merge(xs, gmix, wgate, o_f.reshape(ms, w), o_h.reshape(ms, HG_W), o_m.reshape(ms, w),
                    w_bf, w_bh, w_bm, w_o, ms)
        prev = state_ffn_conv[l]
        xs, ua, ug = _ffn_step(xs, gffn, up, cw, cb, down, prev[:, 0], prev[:, 1], 256)
        conv_s = jnp.stack([prev[:, 1], jnp.concatenate([ua, ug], axis=-1)], axis=1)
        new_s.append((fk.reshape(db, ts, N_HEADS, HEAD_DIM), fv.reshape(db, ts, N_HEADS, HEAD_DIM),
                      lft[:N_HEADS].T.reshape(db, ts, N_HEADS),
                      mk.reshape(db, ts, N_HEADS, HEAD_DIM), mv.reshape(db, ts, N_HEADS, HEAD_DIM),
                      s_s, conv_s))

    outs_p = [jnp.stack(a) for a in zip(*new_p)]
    outs_s = [jnp.stack(a) for a in zip(*new_s)]
    return (xp.reshape(bp, tp, d), xs.reshape(db, ts, d), *outs_p, *outs_s)
```

```python
import functools

import numpy as np
import jax
import jax.numpy as jnp
from jax import lax
from jax.experimental import pallas as pl
from jax.experimental.pallas import tpu as pltpu

F32 = jnp.float32
BF16 = jnp.bfloat16

HEAD_DIM = 64
N_HEADS = 8
ATT_W = N_HEADS * HEAD_DIM
HG_HEADS = 4
HG_D = 128
HG_W = HG_HEADS * HG_D
MOBA_BLOCK = 256
MOBA_TOPK = 3
MOBA_GATE_ROWS = 16
PAGE = 128
N_BRANCH = 3
CONV_W = 3
ROPE_THETA = 10000.0
EPS = 1e-6
NEG = -1e30
REMOVED = -3e38
LOG2E = 1.4426950408889634

LANES = 128
SUBLANES = 8
VMEM_LIMIT = 52 * 1024 * 1024

HG_CHUNK = 64
HG_SUB = 16
DEC_PAGES_PER_STEP = 8
FLASH_CHUNK = 8 * MOBA_BLOCK


def _cp(sem):
    return pltpu.CompilerParams(dimension_semantics=sem, vmem_limit_bytes=VMEM_LIMIT)


def _dot(a, b):
    return jnp.dot(a.astype(BF16), b.astype(BF16), preferred_element_type=F32)


def _dot_nt(a, b):
    return lax.dot_general(a.astype(BF16), b.astype(BF16), (((1,), (1,)), ((), ())),
                           preferred_element_type=F32)


def _dot_tn(a, b):
    return lax.dot_general(a.astype(BF16), b.astype(BF16), (((0,), (0,)), ((), ())),
                           preferred_element_type=F32)


def _pieces(x, n):
    out, r = [], x
    for _ in range(n):
        p = r.astype(BF16)
        out.append(p)
        r = r - p.astype(F32)
    return out


def _rmsnorm(x, g):
    return x * lax.rsqrt(jnp.mean(x * x, axis=-1, keepdims=True) + EPS) * g


def _sigmoid(z):
    return 1.0 / (1.0 + jnp.exp(-z))


def _iota(shape, dim):
    return lax.broadcasted_iota(jnp.int32, shape, dim)


def _attn_proj_kernel(x_ref, g_ref, wfox_ref, wff_ref, fb_ref, wmoba_ref, gm_ref,
                      fqn_ref, fkn_ref, mqn_ref, mkn_ref, cos_ref, sin_ref,
                      fq_ref, fk_ref, fv_ref, lf_ref, mq_ref, mk_ref, mv_ref):
    hb = _rmsnorm(x_ref[...], g_ref[...]).astype(BF16)
    gm = gm_ref[...]

    def headnorm(z, gain):
        hi, lo = _pieces(z * z, 2)
        ms = (jnp.dot(hi, gm, preferred_element_type=F32)
              + jnp.dot(lo, gm, preferred_element_type=F32))
        return z * lax.rsqrt(ms + EPS) * gain

    w = ATT_W
    zf = jnp.dot(hb, wfox_ref[...], preferred_element_type=F32)
    fq_ref[...] = headnorm(zf[:, :w], fqn_ref[...])
    fk_ref[...] = headnorm(zf[:, w:2 * w], fkn_ref[...])
    fv_ref[...] = zf[:, 2 * w:]

    zl = lax.dot_general(wff_ref[...], hb, (((1,), (1,)), ((), ())),
                         preferred_element_type=F32) + fb_ref[...]
    lf_ref[...] = jnp.minimum(zl, 0.0) - jnp.log(1.0 + jnp.exp(-jnp.abs(zl)))

    cos = cos_ref[...]
    sin = sin_ref[...]
    first_half = (_iota((1, w), 1) & (HEAD_DIM - 1)) < HEAD_DIM // 2

    def rope(y):
        swapped = jnp.where(first_half, pltpu.roll(y, w - HEAD_DIM // 2, 1),
                            pltpu.roll(y, HEAD_DIM // 2, 1))
        return y * cos + swapped * sin

    zm = jnp.dot(hb, wmoba_ref[...], preferred_element_type=F32)
    mq_ref[...] = rope(headnorm(zm[:, :w], mqn_ref[...]))
    mk_ref[...] = rope(headnorm(zm[:, w:2 * w], mkn_ref[...]))
    mv_ref[...] = zm[:, 2 * w:]


def _attn_proj(x, gain, wfox, wff, fb, wmoba, gmat, fqn, fkn, mqn, mkn, cos, sin, tm):
    m, d = x.shape
    w = ATT_W
    t_blocks = cos.shape[0] // tm
    row = lambda i: (i, 0)
    const = lambda i: (0, 0)
    big = pl.BlockSpec((tm, w), row)
    out_shape = [jax.ShapeDtypeStruct((m, w), F32)] * 3 + [jax.ShapeDtypeStruct((16, m), F32)] \
        + [jax.ShapeDtypeStruct((m, w), F32)] * 3
    return pl.pallas_call(
        _attn_proj_kernel,
        grid=(m // tm,),
        in_specs=[pl.BlockSpec((tm, d), row), pl.BlockSpec((1, d), const),
                  pl.BlockSpec((d, 3 * w), const), pl.BlockSpec((16, d), const),
                  pl.BlockSpec((16, 1), const), pl.BlockSpec((d, 3 * w), const),
                  pl.BlockSpec((w, w), const)] + [pl.BlockSpec((1, w), const)] * 4
                 + [pl.BlockSpec((tm, w), lambda i: (i % t_blocks, 0))] * 2,
        out_specs=[big, big, big, pl.BlockSpec((16, tm), lambda i: (0, i)), big, big, big],
        out_shape=out_shape,
        compiler_params=_cp(("parallel",)),
        name="attn_proj",
    )(x, gain, wfox, wff, fb, wmoba, gmat, fqn, fkn, mqn, mkn, cos, sin)


def _attn_proj_t_kernel(x_ref, g_ref, wq_ref, wkv_ref, wff_ref, fb_ref, gm_ref,
                        fqn_ref, mqn_ref, fkn_ref, mkn_ref, cos_ref, sin_ref, cost_ref, sint_ref,
                        fq_ref, mq_ref, lf_ref, fk_ref, fv_ref, mk_ref, mv_ref, *bf_refs):
    hb = _rmsnorm(x_ref[...], g_ref[...]).astype(BF16)
    gm = gm_ref[...]
    w = ATT_W
    half = HEAD_DIM // 2

    def headnorm(z, gain):
        hi, lo = _pieces(z * z, 2)
        ms = (jnp.dot(hi, gm, preferred_element_type=F32)
              + jnp.dot(lo, gm, preferred_element_type=F32))
        return z * lax.rsqrt(ms + EPS) * gain

    first_half = (_iota((1, w), 1) & (HEAD_DIM - 1)) < half

    def rope(y):
        swapped = jnp.where(first_half, pltpu.roll(y, w - half, 1), pltpu.roll(y, half, 1))
        return y * cos_ref[...] + swapped * sin_ref[...]

    zq = jnp.dot(hb, wq_ref[...], preferred_element_type=F32)
    fq_ref[...] = headnorm(zq[:, :w], fqn_ref[...])
    mq_ref[...] = rope(headnorm(zq[:, w:], mqn_ref[...]))

    nt = (((1,), (1,)), ((), ()))
    zl = lax.dot_general(wff_ref[...], hb, nt, preferred_element_type=F32) + fb_ref[...]
    lf_ref[...] = jnp.minimum(zl, 0.0) - jnp.log(1.0 + jnp.exp(-jnp.abs(zl)))

    def headnorm_t(z, gain):
        z3 = z.reshape(N_HEADS, HEAD_DIM, z.shape[1])
        ms = jnp.mean(z3 * z3, axis=1, keepdims=True)
        gain_wide = jnp.concatenate([gain] * (z.shape[1] // LANES), axis=1)
        return z3 * lax.rsqrt(ms + EPS) * gain_wide[None]

    def rope_t(y3):
        x1, x2 = y3[:, :half], y3[:, half:]
        cos, sin = cost_ref[...][None], sint_ref[...][None]
        return jnp.concatenate([x1 * cos - x2 * sin, x2 * cos + x1 * sin], axis=1)

    tm = hb.shape[0]
    zkv = lax.dot_general(wkv_ref[...], hb, nt, preferred_element_type=F32)
    outs = (headnorm_t(zkv[:w], fkn_ref[...]).reshape(w, tm), zkv[w:2 * w],
            rope_t(headnorm_t(zkv[2 * w:3 * w], mkn_ref[...])).reshape(w, tm), zkv[3 * w:])
    for val, ref, ref_b in zip(outs, (fk_ref, fv_ref, mk_ref, mv_ref), bf_refs):
        ref[...] = val
        ref_b[...] = val.astype(BF16)


def _attn_proj_t(x, gain, wq, wkv_t, wff, fb, gmat, fqn, mqn, fkn_col, mkn_col,
                 cos, sin, cos_t, sin_t, b, t, tm):
    m, d = x.shape
    w = ATT_W
    tiles = t // tm
    row = lambda i: (i, 0)
    const = lambda i: (0, 0)
    nat = pl.BlockSpec((tm, w), row)
    tr = pl.BlockSpec((None, w, tm), lambda i: (i // tiles, 0, i % tiles))
    return pl.pallas_call(
        _attn_proj_t_kernel,
        grid=(m // tm,),
        in_specs=[pl.BlockSpec((tm, d), row), pl.BlockSpec((1, d), const),
                  pl.BlockSpec((d, 2 * w), const), pl.BlockSpec((4 * w, d), const),
                  pl.BlockSpec((16, d), const), pl.BlockSpec((16, 1), const),
                  pl.BlockSpec((w, w), const), pl.BlockSpec((1, w), const),
                  pl.BlockSpec((1, w), const), pl.BlockSpec((HEAD_DIM, LANES), const),
                  pl.BlockSpec((HEAD_DIM, LANES), const),
                  pl.BlockSpec((tm, w), lambda i: (i % tiles, 0)),
                  pl.BlockSpec((tm, w), lambda i: (i % tiles, 0)),
                  pl.BlockSpec((HEAD_DIM // 2, tm), lambda i: (0, i % tiles)),
                  pl.BlockSpec((HEAD_DIM // 2, tm), lambda i: (0, i % tiles))],
        out_specs=[nat, nat, pl.BlockSpec((16, tm), lambda i: (0, i))] + [tr] * 8,
        out_shape=[jax.ShapeDtypeStruct((m, w), F32)] * 2 + [jax.ShapeDtypeStruct((16, m), F32)]
                  + [jax.ShapeDtypeStruct((b, w, t), F32)] * 4
                  + [jax.ShapeDtypeStruct((b, w, t), BF16)] * 4,
        compiler_params=_cp(("parallel",)),
        name="attn_proj_prompt",
    )(x, gain, wq, wkv_t, wff, fb, gmat, fqn, mqn, fkn_col, mkn_col, cos, sin, cos_t, sin_t)


def _hgrn_proj_kernel(x_ref, g_ref, w_ref, lb_ref, hq_ref, lf_ref, kin_ref, hi_ref, hg_ref):
    hb = _rmsnorm(x_ref[...], g_ref[...]).astype(BF16)
    z = jnp.dot(hb, w_ref[...], preferred_element_type=F32)
    w = HG_W
    zq = z[:, :w]
    hq_ref[...] = zq * _sigmoid(zq)
    zf = z[:, w:2 * w]
    lb = lb_ref[...]
    lf_ref[...] = jnp.log(lb + (1.0 - lb) * _sigmoid(zf))
    kin_ref[...] = (1.0 - lb) * _sigmoid(-zf)
    hi_ref[...] = z[:, 2 * w:3 * w]
    hg_ref[...] = _sigmoid(z[:, 3 * w:])


def _hgrn_proj(x, gain, w_h, lb, tm):
    m, d = x.shape
    w = HG_W
    row = lambda i: (i, 0)
    const = lambda i: (0, 0)
    return pl.pallas_call(
        _hgrn_proj_kernel,
        grid=(m // tm,),
        in_specs=[pl.BlockSpec((tm, d), row), pl.BlockSpec((1, d), const),
                  pl.BlockSpec((d, 4 * w), const), pl.BlockSpec((1, w), const)],
        out_specs=[pl.BlockSpec((tm, w), row)] * 5,
        out_shape=[jax.ShapeDtypeStruct((m, w), F32)] * 5,
        compiler_params=_cp(("parallel",)),
        name="hgrn_proj",
    )(x, gain, w_h, lb)


def _cumsum_kernel(x_ref, o_ref):
    x = x_ref[...]
    n = x.shape[1]
    lane = _iota(x.shape, 1)
    s = 1
    while s < n:
        x = x + jnp.where(lane >= s, pltpu.roll(x, s, 1), 0.0)
        s *= 2
    o_ref[...] = x * LOG2E


def _cumsum_rows(x, seg):
    r, n = x.shape
    return pl.pallas_call(
        _cumsum_kernel,
        grid=(n // seg,),
        in_specs=[pl.BlockSpec((r, seg), lambda b: (0, b))],
        out_specs=pl.BlockSpec((r, seg), lambda b: (0, b)),
        out_shape=jax.ShapeDtypeStruct((r, n), F32),
        compiler_params=_cp(("parallel",)),
        name="fox_cumsum",
    )(x)


def _chunk_tables(n_q, per_chunk):
    i_tab = np.concatenate([np.full(i // per_chunk + 1, i, np.int32) for i in range(n_q)])
    c_tab = np.concatenate([np.arange(i // per_chunk + 1, dtype=np.int32) for i in range(n_q)])
    return jnp.asarray(i_tab), jnp.asarray(c_tab)


def _store_head_queries(q, qs_ref, scale):
    odd = _iota((1, LANES), 1) >= HEAD_DIM
    for h in range(N_HEADS):
        grp = q[:, (h // 2) * LANES:(h // 2 + 1) * LANES] * scale
        keep = odd if h % 2 else jnp.logical_not(odd)
        qs_ref[h] = jnp.where(keep, grp, 0.0).astype(BF16)


def _flash_update(h, s, vb_t, m_ref, l_ref, acc_ref, visible=None):
    m_old = m_ref[h]
    m_new = jnp.maximum(m_old, jnp.max(s, axis=-1, keepdims=True))
    p = jnp.exp2(s - m_new[:, :1])
    if visible is not None:
        p = jnp.where(visible, p, 0.0)
    alpha = jnp.exp2(m_old - m_new)
    l_ref[h] = alpha * l_ref[h] + jnp.sum(p, axis=-1, keepdims=True)
    pv = lax.dot_general(p.astype(BF16), vb_t, (((1,), (1,)), ((), ())),
                         preferred_element_type=F32)
    acc_ref[h] = alpha * acc_ref[h] + pv
    m_ref[h] = m_new


def _flash_finish(o_ref, l_ref, acc_ref):
    odd = _iota((1, LANES), 1) >= HEAD_DIM
    for g in range(N_HEADS // 2):
        a0 = acc_ref[2 * g] / l_ref[2 * g]
        a1 = acc_ref[2 * g + 1] / l_ref[2 * g + 1]
        o_ref[:, g * LANES:(g + 1) * LANES] = jnp.where(odd, a1, a0)


def _flash_init(q_ref, qs_ref, m_ref, l_ref, acc_ref):
    _store_head_queries(q_ref[...], qs_ref, HEAD_DIM ** -0.5 * LOG2E)
    m_ref[...] = jnp.full(m_ref.shape, NEG, F32)
    l_ref[...] = jnp.zeros(l_ref.shape, F32)
    acc_ref[...] = jnp.zeros(acc_ref.shape, F32)


def _fox_kernel(itab, ctab, q_ref, k_ref, v_ref, c_ref, o_ref, qs_ref, m_ref, l_ref, acc_ref,
                *, per_chunk):
    p = pl.program_id(1)
    i = itab[p]
    c = ctab[p]
    tq = q_ref.shape[0]
    tk = k_ref.shape[1]

    @pl.when(c == 0)
    def _():
        _flash_init(q_ref, qs_ref, m_ref, l_ref, acc_ref)

    def step(width, diagonal_tail):
        causal = _iota((tq, tq), 1) <= _iota((tq, tq), 0)
        for g in range(N_HEADS // 2):
            rows = slice(g * LANES, (g + 1) * LANES)
            kb = k_ref[rows, :width]
            vb = v_ref[rows, :width]
            for h in (2 * g, 2 * g + 1):
                s = jnp.dot(qs_ref[h], kb, preferred_element_type=F32) - c_ref[h:h + 1, :width]
                if diagonal_tail:
                    tail = jnp.where(causal, s[:, width - tq:], NEG)
                    s = tail if width == tq else jnp.concatenate([s[:, :width - tq], tail], axis=1)
                _flash_update(h, s, vb, m_ref, l_ref, acc_ref)

    last = i // per_chunk

    @pl.when(c < last)
    def _():
        step(tk, False)

    for r in range(per_chunk):
        @pl.when(jnp.logical_and(c == last, i % per_chunk == r))
        def _(r=r):
            step((r + 1) * tq, True)
            _flash_finish(o_ref, l_ref, acc_ref)


def _fox_prompt(q, k_t, v_t, c, b, t, tq, tk):
    nq = t // tq
    nc = t // tk
    per_chunk = tk // tq
    itab, ctab = _chunk_tables(nq, per_chunk)
    w = ATT_W
    qmap = lambda bb, p, it, ct: (bb * nq + it[p], 0)
    kmap = lambda bb, p, it, ct: (bb, 0, ct[p])
    grid_spec = pltpu.PrefetchScalarGridSpec(
        num_scalar_prefetch=2,
        grid=(b, int(itab.shape[0])),
        in_specs=[pl.BlockSpec((tq, w), qmap),
                  pl.BlockSpec((None, w, tk), kmap),
                  pl.BlockSpec((None, w, tk), kmap),
                  pl.BlockSpec((16, tk), lambda bb, p, it, ct: (0, bb * nc + ct[p]))],
        out_specs=pl.BlockSpec((tq, w), qmap),
        scratch_shapes=[pltpu.VMEM((N_HEADS, tq, LANES), BF16),
                        pltpu.VMEM((N_HEADS, tq, LANES), F32),
                        pltpu.VMEM((N_HEADS, tq, LANES), F32),
                        pltpu.VMEM((N_HEADS, tq, LANES), F32)])
    return pl.pallas_call(
        functools.partial(_fox_kernel, per_chunk=per_chunk), grid_spec=grid_spec,
        out_shape=jax.ShapeDtypeStruct((b * t, w), F32),
        compiler_params=_cp(("parallel", "arbitrary")),
        name="fox_prompt",
    )(itab, ctab, q, k_t, v_t, c)


def _block_mean_kernel(k_ref, o_ref):
    rows = o_ref.shape[0]
    t = k_ref.shape[1]
    row = _iota((rows, MOBA_BLOCK), 0)
    acc = jnp.zeros(o_ref.shape, F32)
    for n in range(t // MOBA_BLOCK):
        avg = jnp.where(row == n, 1.0 / MOBA_BLOCK, 0.0).astype(BF16)
        for part in _pieces(k_ref[:, n * MOBA_BLOCK:(n + 1) * MOBA_BLOCK], 3):
            acc = acc + lax.dot_general(avg, part, (((1,), (1,)), ((), ())),
                                        preferred_element_type=F32)
    o_ref[...] = acc


def _block_means(k_t):
    b, w, t = k_t.shape
    return pl.pallas_call(
        _block_mean_kernel,
        grid=(b,),
        in_specs=[pl.BlockSpec((None, w, t), lambda bb: (bb, 0, 0))],
        out_specs=pl.BlockSpec((None, MOBA_GATE_ROWS, w), lambda bb: (bb, 0, 0)),
        out_shape=jax.ShapeDtypeStruct((b, MOBA_GATE_ROWS, w), F32),
        compiler_params=_cp(("parallel",)),
        name="moba_block_means",
    )(k_t)


def _top_blocks_t(g_t, n_valid):
    nb = MOBA_GATE_ROWS
    row = _iota((nb, g_t.shape[1]), 0)
    out = []
    for h in range(N_HEADS):
        g = jnp.where(row < n_valid, g_t[h * nb:(h + 1) * nb], NEG)
        sel = jnp.zeros(g.shape, F32)
        for _ in range(MOBA_TOPK):
            mx = jnp.max(g, axis=0, keepdims=True)
            idx = jnp.min(jnp.where(g == mx, row, nb), axis=0, keepdims=True)
            hit = row == idx
            sel = jnp.where(jnp.logical_and(hit, mx > 0.5 * NEG), 1.0, sel)
            g = jnp.where(hit, REMOVED, g)
        out.append(sel)
    return jnp.concatenate(out, axis=0)


def _moba_kernel(itab, ctab, q_ref, k_ref, v_ref, km_ref, o_ref,
                 qs_ref, sel_ref, m_ref, l_ref, acc_ref, *, per_chunk):
    p = pl.program_id(1)
    i = itab[p]
    c = ctab[p]
    tq = q_ref.shape[0]
    nb = MOBA_GATE_ROWS

    @pl.when(c == 0)
    def _():
        _flash_init(q_ref, qs_ref, m_ref, l_ref, acc_ref)
        shape = (N_HEADS * nb, ATT_W)
        km_heads = jnp.where(_iota(shape, 0) // nb == _iota(shape, 1) // HEAD_DIM,
                             jnp.concatenate([km_ref[...]] * N_HEADS, axis=0), 0.0)
        kh, kl = _pieces(km_heads, 2)
        qh, ql = _pieces(q_ref[...], 2)
        nt = (((1,), (1,)), ((), ()))
        g_t = (lax.dot_general(kh, qh, nt, preferred_element_type=F32)
               + lax.dot_general(kh, ql, nt, preferred_element_type=F32)
               + lax.dot_general(kl, qh, nt, preferred_element_type=F32))
        sel_ref[...] = _top_blocks_t(g_t, i).T

    def step(n_blocks, own_tail):
        causal = _iota((tq, tq), 1) <= _iota((tq, tq), 0)
        sel = sel_ref[...]
        col = _iota(sel.shape, 1)
        width = n_blocks * tq
        for g in range(N_HEADS // 2):
            rows = slice(g * LANES, (g + 1) * LANES)
            kb = k_ref[rows, :width]
            vb = v_ref[rows, :width]
            for h in (2 * g, 2 * g + 1):
                s = jnp.dot(qs_ref[h], kb, preferred_element_type=F32)
                vis = []
                for n in range(n_blocks):
                    if own_tail and n == n_blocks - 1:
                        vis.append(causal)
                    else:
                        blk = c * per_chunk + n
                        picked = jnp.sum(jnp.where(col == h * nb + blk, sel, 0.0),
                                         axis=-1, keepdims=True)
                        vis.append(jnp.broadcast_to(picked > 0.5, (tq, tq)))
                visible = vis[0] if n_blocks == 1 else jnp.concatenate(vis, axis=1)
                _flash_update(h, jnp.where(visible, s, NEG), vb, m_ref, l_ref, acc_ref,
                              visible=visible)

    last = i // per_chunk

    @pl.when(c < last)
    def _():
        step(per_chunk, False)

    for r in range(per_chunk):
        @pl.when(jnp.logical_and(c == last, i % per_chunk == r))
        def _(r=r):
            step(r + 1, True)
            _flash_finish(o_ref, l_ref, acc_ref)


def _moba_prompt(q, k_t, v_t, km, b, t, tk):
    tq = MOBA_BLOCK
    nb = t // tq
    assert nb <= MOBA_GATE_ROWS and N_HEADS * MOBA_GATE_ROWS == LANES
    per_chunk = tk // tq
    itab, ctab = _chunk_tables(nb, per_chunk)
    w = ATT_W
    qmap = lambda bb, p, it, ct: (bb * nb + it[p], 0)
    kmap = lambda bb, p, it, ct: (bb, 0, ct[p])
    grid_spec = pltpu.PrefetchScalarGridSpec(
        num_scalar_prefetch=2,
        grid=(b, int(itab.shape[0])),
        in_specs=[pl.BlockSpec((tq, w), qmap),
                  pl.BlockSpec((None, w, tk), kmap),
                  pl.BlockSpec((None, w, tk), kmap),
                  pl.BlockSpec((None, MOBA_GATE_ROWS, w), lambda bb, p, it, ct: (bb, 0, 0))],
        out_specs=pl.BlockSpec((tq, w), qmap),
        scratch_shapes=[pltpu.VMEM((N_HEADS, tq, LANES), BF16),
                        pltpu.VMEM((tq, LANES), F32),
                        pltpu.VMEM((N_HEADS, tq, LANES), F32),
                        pltpu.VMEM((N_HEADS, tq, LANES), F32),
                        pltpu.VMEM((N_HEADS, tq, LANES), F32)])
    return pl.pallas_call(
        functools.partial(_moba_kernel, per_chunk=per_chunk), grid_spec=grid_spec,
        out_shape=jax.ShapeDtypeStruct((b * t, w), F32),
        compiler_params=_cp(("parallel", "arbitrary")),
        name="moba_prompt",
    )(itab, ctab, q, k_t, v_t, km)


def _hgrn_kernel(q_ref, f_ref, k_ref, v_ref, g_ref, on_ref, o_ref, s_ref, st_ref, *, n_chunks):
    tstep = pl.program_id(2)
    C, c = HG_CHUNK, HG_SUB

    @pl.when(tstep == 0)
    def _():
        st_ref[...] = jnp.zeros(st_ref.shape, F32)

    ltri = (_iota((C, C), 0) >= _iota((C, C), 1)).astype(BF16)
    trow = _iota((c, HG_D), 0)

    def chunk(ci, carry):
        r0 = pl.multiple_of(ci * C, C)
        q = q_ref[pl.ds(r0, C), :]
        lf = f_ref[pl.ds(r0, C), :]
        kin = k_ref[pl.ds(r0, C), :]
        v = v_ref[pl.ds(r0, C), :]
        b = sum(jnp.dot(ltri, part, preferred_element_type=F32) for part in _pieces(lf, 3))
        st = st_ref[...]
        o_inter = _dot_nt(q * jnp.exp(b), st)
        b_last = b[C - 1:C, :]
        st_ref[...] = st * jnp.exp(b_last) + _dot_tn(v, kin * jnp.exp(b_last - b))

        out = [o_inter[n * c:(n + 1) * c] for n in range(C // c)]
        for jj in range(C // c):
            rows = slice(jj * c, (jj + 1) * c)
            qj, kj, bj, vj = q[rows], kin[rows], b[rows], v[rows]
            acc = jnp.zeros((c, HG_D), F32)
            for s in range(c):
                decay = jnp.exp(jnp.where(trow >= s, bj - bj[s:s + 1], NEG))
                a = jnp.sum(qj * kj[s:s + 1] * decay, axis=-1, keepdims=True)
                acc = acc + a * vj[s:s + 1]
            out[jj] = out[jj] + acc
            if jj + 1 < C // c:
                b_end = bj[c - 1:c]
                kp = kj * jnp.exp(b_end - bj)
                qp = q[(jj + 1) * c:] * jnp.exp(b[(jj + 1) * c:] - b_end)
                oo = _dot(_dot_nt(qp, kp), vj)
                for n in range(jj + 1, C // c):
                    out[n] = out[n] + oo[(n - jj - 1) * c:(n - jj) * c]
        o = jnp.concatenate(out, axis=0)
        o_ref[pl.ds(r0, C), :] = _rmsnorm(o, on_ref[...]) * g_ref[pl.ds(r0, C), :]
        return carry

    lax.fori_loop(0, n_chunks, chunk, 0, unroll=4)

    @pl.when(tstep == pl.num_programs(2) - 1)
    def _():
        s_ref[...] = st_ref[...].T


def _hgrn_prompt(q, lf, kin, v, gate, onorm, b, t, tb):
    nt = t // tb
    blk = pl.BlockSpec((tb, HG_D), lambda bb, h, tt: (bb * nt + tt, h))
    return pl.pallas_call(
        functools.partial(_hgrn_kernel, n_chunks=tb // HG_CHUNK),
        grid=(b, HG_HEADS, nt),
        in_specs=[blk] * 5 + [pl.BlockSpec((1, HG_D), lambda bb, h, tt: (0, 0))],
        out_specs=[blk, pl.BlockSpec((None, None, HG_D, HG_D), lambda bb, h, tt: (bb, h, 0, 0))],
        out_shape=[jax.ShapeDtypeStruct((b * t, HG_W), F32),
                   jax.ShapeDtypeStruct((b, HG_HEADS, HG_D, HG_D), F32)],
        scratch_shapes=[pltpu.VMEM((HG_D, HG_D), F32)],
        compiler_params=_cp(("parallel", "parallel", "arbitrary")),
        name="hgrn_prompt",
    )(q, lf, kin, v, gate, onorm)


def _merge_kernel(x_ref, g_ref, wg_ref, of_ref, oh_ref, om_ref, wf_ref, wh_ref, wm_ref,
                  wo_ref, y_ref):
    x = x_ref[...]
    d = x.shape[1]
    hb = _rmsnorm(x, g_ref[...]).astype(BF16)
    gates = _sigmoid(jnp.dot(hb, wg_ref[...], preferred_element_type=F32))
    y = (gates[:, :d] * _dot(of_ref[...], wf_ref[...])
         + gates[:, d:2 * d] * _dot(oh_ref[...], wh_ref[...])
         + gates[:, 2 * d:] * _dot(om_ref[...], wm_ref[...]))
    y_ref[...] = x + _dot(y, wo_ref[...])


def _merge(x, gain, wg, o_f, o_h, o_m, w_bf, w_bh, w_bm, w_out, tm):
    m, d = x.shape
    w = ATT_W
    row = lambda i: (i, 0)
    const = lambda i: (0, 0)
    return pl.pallas_call(
        _merge_kernel,
        grid=(m // tm,),
        in_specs=[pl.BlockSpec((tm, d), row), pl.BlockSpec((1, d), const),
                  pl.BlockSpec((d, N_BRANCH * d), const)]
                 + [pl.BlockSpec((tm, w), row)] * 3 + [pl.BlockSpec((w, d), const)] * 3
                 + [pl.BlockSpec((d, d), const)],
        out_specs=pl.BlockSpec((tm, d), row),
        out_shape=jax.ShapeDtypeStruct((m, d), F32),
        compiler_params=_cp(("parallel",)),
        name="branch_merge",
    )(x, gain, wg, o_f, o_h, o_m, w_bf, w_bh, w_bm, w_out)


def _ffn_kernel(x_ref, g_ref, wa_ref, wg_ref, cwa_ref, cwg_ref, cba_ref, cbg_ref, wd_ref,
                y_ref, ca_ref, cg_ref, hb_ref, acc_ref, carry_ref, *, tiles_per_seq):
    i = pl.program_id(0)
    f = pl.program_id(1)
    tm = x_ref.shape[0]

    @pl.when(f == 0)
    def _():
        hb_ref[...] = _rmsnorm(x_ref[...], g_ref[...]).astype(BF16)
        acc_ref[...] = jnp.zeros(acc_ref.shape, F32)

    hb = hb_ref[...]
    rowi = _iota((tm, wa_ref.shape[1]), 0)

    @pl.when((i % tiles_per_seq) == 0)
    def _():
        carry_ref[f] = jnp.zeros(carry_ref.shape[1:], F32)

    def conv(u, cw_ref, cb_ref, slot, tail_ref):
        prev = carry_ref[f, slot]
        p1 = prev[SUBLANES - 1:SUBLANES]
        p2 = prev[SUBLANES - 2:SUBLANES - 1]
        u1 = jnp.where(rowi == 0, p1, pltpu.roll(u, 1, 0))
        u2 = jnp.where(rowi == 0, p2, jnp.where(rowi == 1, p1, pltpu.roll(u, 2, 0)))
        cw = cw_ref[...]
        tail = u[tm - SUBLANES:]
        carry_ref[f, slot] = tail
        tail_ref[...] = tail
        return cb_ref[...] + cw[0:1] * u2 + cw[1:2] * u1 + cw[2:3] * u

    a = conv(jnp.dot(hb, wa_ref[...], preferred_element_type=F32), cwa_ref, cba_ref, 0, ca_ref)
    g = conv(jnp.dot(hb, wg_ref[...], preferred_element_type=F32), cwg_ref, cbg_ref, 1, cg_ref)
    acc_ref[...] += _dot(a * _sigmoid(a) * g, wd_ref[...])

    @pl.when(f == pl.num_programs(1) - 1)
    def _():
        y_ref[...] = x_ref[...] + acc_ref[...]


def _ffn_prompt(x, gain, w_up, conv_w, conv_b, w_down, b, t, tm, fc):
    m, d = x.shape
    dff = w_down.shape[0]
    nf = dff // fc
    tiles = t // tm
    return pl.pallas_call(
        functools.partial(_ffn_kernel, tiles_per_seq=tiles),
        grid=(m // tm, nf),
        in_specs=[pl.BlockSpec((tm, d), lambda i, f: (i, 0)),
                  pl.BlockSpec((1, d), lambda i, f: (0, 0)),
                  pl.BlockSpec((d, fc), lambda i, f: (0, f)),
                  pl.BlockSpec((d, fc), lambda i, f: (0, nf + f)),
                  pl.BlockSpec((CONV_W, fc), lambda i, f: (0, f)),
                  pl.BlockSpec((CONV_W, fc), lambda i, f: (0, nf + f)),
                  pl.BlockSpec((1, fc), lambda i, f: (0, f)),
                  pl.BlockSpec((1, fc), lambda i, f: (0, nf + f)),
                  pl.BlockSpec((fc, d), lambda i, f: (f, 0))],
        out_specs=[pl.BlockSpec((tm, d), lambda i, f: (i, 0)),
                   pl.BlockSpec((None, SUBLANES, fc), lambda i, f: (i, 0, f)),
                   pl.BlockSpec((None, SUBLANES, fc), lambda i, f: (i, 0, f))],
        out_shape=[jax.ShapeDtypeStruct((m, d), F32),
                   jax.ShapeDtypeStruct((m // tm, SUBLANES, dff), F32),
                   jax.ShapeDtypeStruct((m // tm, SUBLANES, dff), F32)],
        scratch_shapes=[pltpu.VMEM((tm, d), BF16), pltpu.VMEM((tm, d), F32),
                        pltpu.VMEM((nf, 2, SUBLANES, fc), F32)],
        compiler_params=_cp(("arbitrary", "arbitrary")),
        name="conv_ffn_prompt",
    )(x, gain, w_up, w_up, conv_w, conv_w, conv_b, conv_b, w_down)


def _ffn_step_kernel(x_ref, g_ref, wa_ref, wg_ref, cwa_ref, cwg_ref, cba_ref, cbg_ref,
                     pa0_ref, pa1_ref, pg0_ref, pg1_ref, wd_ref, y_ref, ua_ref, ug_ref,
                     hb_ref, acc_ref):
    f = pl.program_id(0)

    @pl.when(f == 0)
    def _():
        hb_ref[...] = _rmsnorm(x_ref[...], g_ref[...]).astype(BF16)
        acc_ref[...] = jnp.zeros(acc_ref.shape, F32)

    hb = hb_ref[...]

    def conv(w_ref, cw_ref, cb_ref, p0_ref, p1_ref, u_ref):
        u = jnp.dot(hb, w_ref[...], preferred_element_type=F32)
        u_ref[...] = u
        cw = cw_ref[...]
        return cb_ref[...] + cw[0:1] * p0_ref[...] + cw[1:2] * p1_ref[...] + cw[2:3] * u

    a = conv(wa_ref, cwa_ref, cba_ref, pa0_ref, pa1_ref, ua_ref)
    g = conv(wg_ref, cwg_ref, cbg_ref, pg0_ref, pg1_ref, ug_ref)
    acc_ref[...] += _dot(a * _sigmoid(a) * g, wd_ref[...])

    @pl.when(f == pl.num_programs(0) - 1)
    def _():
        y_ref[...] = x_ref[...] + acc_ref[...]


def _ffn_step(x, gain, w_up, conv_w, conv_b, w_down, prev0, prev1, fc):
    m, d = x.shape
    dff = w_down.shape[0]
    nf = dff // fc
    lo = lambda f: (0, f)
    hi = lambda f: (0, nf + f)
    return pl.pallas_call(
        _ffn_step_kernel,
        grid=(nf,),
        in_specs=[pl.BlockSpec((m, d), lambda f: (0, 0)), pl.BlockSpec((1, d), lambda f: (0, 0)),
                  pl.BlockSpec((d, fc), lo), pl.BlockSpec((d, fc), hi),
                  pl.BlockSpec((CONV_W, fc), lo), pl.BlockSpec((CONV_W, fc), hi),
                  pl.BlockSpec((1, fc), lo), pl.BlockSpec((1, fc), hi),
                  pl.BlockSpec((m, fc), lo), pl.BlockSpec((m, fc), lo),
                  pl.BlockSpec((m, fc), hi), pl.BlockSpec((m, fc), hi),
                  pl.BlockSpec((fc, d), lambda f: (f, 0))],
        out_specs=[pl.BlockSpec((m, d), lambda f: (0, 0)),
                   pl.BlockSpec((m, fc), lo), pl.BlockSpec((m, fc), lo)],
        out_shape=[jax.ShapeDtypeStruct((m, d), F32), jax.ShapeDtypeStruct((m, dff), F32),
                   jax.ShapeDtypeStruct((m, dff), F32)],
        scratch_shapes=[pltpu.VMEM((m, d), BF16), pltpu.VMEM((m, d), F32)],
        compiler_params=_cp(("arbitrary",)),
        name="conv_ffn_step",
    )(x, gain, w_up, w_up, conv_w, conv_w, conv_b, conv_b, prev0, prev1, prev0, prev1, w_down)


def _decode_scan_kernel(pt_ref, fq_ref, mq_ref, lfn_ref, kn_ref, vn_ref, *rest,
                        pages_per_step, n_blocks):
    pp = pages_per_step
    fk = rest[0:pp]
    fv = rest[pp:2 * pp]
    lf = rest[2 * pp:3 * pp]
    mk = rest[3 * pp:4 * pp]
    o_ref, top_ref, m_ref, l_ref, acc_ref, carry_ref, gate_ref = rest[4 * pp:]
    g = pl.program_id(1)
    d = HEAD_DIM

    @pl.when(g == 0)
    def _():
        m_ref[...] = jnp.full(m_ref.shape, NEG, F32)
        l_ref[...] = jnp.zeros(l_ref.shape, F32)
        acc_ref[...] = jnp.zeros(acc_ref.shape, F32)
        carry_ref[...] = jnp.zeros(carry_ref.shape, F32)
        gate_ref[...] = jnp.zeros(gate_ref.shape, F32)

    lane = _iota((SUBLANES, PAGE), 1)
    utri = (_iota((PAGE, PAGE), 0) <= _iota((PAGE, PAGE), 1)).astype(BF16)

    def head_scores(tile_of, q_ref):
        return jnp.concatenate(
            [jnp.sum(tile_of(h) * q_ref[h * d:(h + 1) * d, :], axis=0, keepdims=True)
             for h in range(N_HEADS)], axis=0)

    def attend(k_of, v_of, c_ts, valid):
        zs = [head_scores(functools.partial(k_of, r), fq_ref) * (HEAD_DIM ** -0.5) - c_t
              for r, c_t in enumerate(c_ts)]
        if valid is not None:
            zs = [jnp.where(valid, z, NEG) for z in zs]
        m_old = m_ref[...]
        m_new = m_old
        for z in zs:
            m_new = jnp.maximum(m_new, jnp.max(z, axis=-1, keepdims=True))
        ps = [jnp.exp(z - m_new) for z in zs]
        if valid is not None:
            ps = [jnp.where(valid, p, 0.0) for p in ps]
        alpha = jnp.exp(m_old - m_new)
        l_ref[...] = alpha * l_ref[...] + sum(jnp.sum(p, axis=-1, keepdims=True) for p in ps)
        m_ref[...] = m_new
        for h in range(N_HEADS):
            acc_ref[h] = alpha[h:h + 1, :] * acc_ref[h] + sum(
                p[h:h + 1, :] * v_of(r, h) for r, p in enumerate(ps))

    rows = []
    for r in range(pp):
        rows += [part.astype(F32) for part in _pieces(lf[r][...], 3)]
    stacked = jnp.concatenate(rows, axis=0).astype(BF16)
    within = jnp.dot(stacked, utri, preferred_element_type=F32)
    totals = jnp.dot(stacked, jnp.ones((PAGE, PAGE), BF16), preferred_element_type=F32)
    fold = lambda y, r: y[24 * r:24 * r + 8] + y[24 * r + 8:24 * r + 16] + y[24 * r + 16:24 * r + 24]
    carry = carry_ref[...]
    c_ts = []
    for r in range(pp):
        c_ts.append(carry + fold(within, r))
        carry = carry + fold(totals, r)
    carry_ref[...] = carry
    attend(lambda r, h: fk[r][h], lambda r, h: fv[r][h], c_ts, None)

    for r in range(pp):
        tot = jnp.sum(head_scores(lambda h: mk[r][h], mq_ref), axis=-1, keepdims=True)
        gate_ref[...] += jnp.where(lane == (g * pp + r) // 2, tot, 0.0)

    @pl.when(g == pl.num_programs(1) - 1)
    def _():
        attend(lambda r, h: kn_ref[h * d:(h + 1) * d, :], lambda r, h: vn_ref[h * d:(h + 1) * d, :],
               [carry_ref[...] + lfn_ref[...]], lane == 0)
        for h in range(N_HEADS):
            tot = jnp.sum(acc_ref[h], axis=-1, keepdims=True)
            o_ref[h * d:(h + 1) * d, :] = jnp.broadcast_to(tot, (d, PAGE)) / l_ref[h:h + 1, :]

        gate = jnp.where(lane < n_blocks, gate_ref[...] * (1.0 / MOBA_BLOCK), REMOVED)
        top = jnp.zeros((SUBLANES, PAGE), jnp.int32)
        for r in range(MOBA_TOPK):
            mx = jnp.max(gate, axis=-1, keepdims=True)
            idx = jnp.min(jnp.where(gate == mx, lane, PAGE), axis=-1, keepdims=True)
            top = jnp.where(lane == r, idx, top)
            gate = jnp.where(lane == idx, REMOVED, gate)
        top_ref[...] = top


def _decode_scan(page_table, layer, fq, mq, lf_new, k_new, v_new,
                 cache_fk, cache_fv, cache_lf, cache_mk):
    db, n_pages = page_table.shape
    pp = DEC_PAGES_PER_STEP
    n_blocks = n_pages * PAGE // MOBA_BLOCK
    w = ATT_W
    seq = lambda rows: pl.BlockSpec((None, rows, PAGE), lambda b, g, pt: (b, 0, 0))

    def page(r):
        return pl.BlockSpec((None, None, N_HEADS, HEAD_DIM, PAGE),
                            lambda b, g, pt, r=r: (layer, pt[b, g * pp + r], 0, 0, 0))

    def gate_page(r):
        return pl.BlockSpec((None, None, N_HEADS, PAGE),
                            lambda b, g, pt, r=r: (layer, pt[b, g * pp + r], 0, 0))

    in_specs = [seq(w), seq(w), seq(N_HEADS), seq(w), seq(w)]
    in_specs += [page(r) for r in range(pp)] * 2 + [gate_page(r) for r in range(pp)]
    in_specs += [page(r) for r in range(pp)]
    grid_spec = pltpu.PrefetchScalarGridSpec(
        num_scalar_prefetch=1, grid=(db, n_pages // pp), in_specs=in_specs,
        out_specs=[seq(w), seq(N_HEADS)],
        scratch_shapes=[pltpu.VMEM((N_HEADS, PAGE), F32), pltpu.VMEM((N_HEADS, PAGE), F32),
                        pltpu.VMEM((N_HEADS, HEAD_DIM, PAGE), F32),
                        pltpu.VMEM((N_HEADS, PAGE), F32), pltpu.VMEM((N_HEADS, PAGE), F32)])
    return pl.pallas_call(
        functools.partial(_decode_scan_kernel, pages_per_step=pp, n_blocks=n_blocks),
        grid_spec=grid_spec,
        out_shape=[jax.ShapeDtypeStruct((db, w, PAGE), F32),
                   jax.ShapeDtypeStruct((db, N_HEADS, PAGE), jnp.int32)],
        compiler_params=_cp(("parallel", "arbitrary")),
        name="decode_scan",
    )(page_table, fq, mq, lf_new, k_new, v_new,
      *([cache_fk] * pp), *([cache_fv] * pp), *([cache_lf] * pp), *([cache_mk] * pp))


def _moba_step_kernel(pt_ref, top_ref, q_ref, kn_ref, vn_ref, *rest):
    n = 2 * MOBA_TOPK
    kp = rest[0:n]
    vp = rest[n:2 * n]
    o_ref = rest[2 * n]
    q8 = jnp.broadcast_to(q_ref[...] * (HEAD_DIM ** -0.5), (SUBLANES, HEAD_DIM))
    s_own = jnp.sum(q8 * kn_ref[...], axis=-1, keepdims=True)
    scores = [_dot(q8, kp[r][...]) for r in range(n)]
    m = s_own
    for s in scores:
        m = jnp.maximum(m, jnp.max(s, axis=-1, keepdims=True))
    p_own = jnp.exp(s_own - m)
    l = p_own
    acc = p_own * vn_ref[...]
    for r, s in enumerate(scores):
        p = jnp.exp(s - m)
        l = l + jnp.sum(p, axis=-1, keepdims=True)
        acc = acc + _dot_nt(p, vp[r][...])
    o_ref[...] = (acc / l)[0:1]


def _moba_step(page_table, top_flat, layer, mq, k_new, v_new, cache_mk, cache_mv):
    db = page_table.shape[0]
    vec = pl.BlockSpec((None, None, 1, HEAD_DIM), lambda b, h, pt, tp: (b, h, 0, 0))

    def page(r, half):
        def index(b, h, pt, tp):
            blk = tp[(b * N_HEADS + h) * MOBA_TOPK + r]
            return (layer, pt[b, 2 * blk + half], h, 0, 0)
        return pl.BlockSpec((None, None, None, HEAD_DIM, PAGE), index)

    pages = [page(r, half) for r in range(MOBA_TOPK) for half in range(2)]
    grid_spec = pltpu.PrefetchScalarGridSpec(
        num_scalar_prefetch=2, grid=(db, N_HEADS),
        in_specs=[vec, vec, vec] + pages + pages,
        out_specs=vec)
    n = len(pages)
    return pl.pallas_call(
        _moba_step_kernel, grid_spec=grid_spec,
        out_shape=jax.ShapeDtypeStruct((db, N_HEADS, 1, HEAD_DIM), F32),
        compiler_params=_cp(("parallel", "parallel")),
        name="moba_step",
    )(page_table, top_flat, mq, k_new, v_new, *([cache_mk] * n), *([cache_mv] * n))


def _hgrn_step_kernel(q_ref, f_ref, k_ref, v_ref, g_ref, on_ref, s0_ref, o_ref, s_ref):
    def column(row):
        return jnp.broadcast_to(row, (HG_D, HG_D)).T

    s_new = column(jnp.exp(f_ref[...])) * s0_ref[...] + column(k_ref[...]) * v_ref[...]
    s_ref[...] = s_new
    o = jnp.sum(column(q_ref[...]) * s_new, axis=0, keepdims=True)
    o_ref[...] = _rmsnorm(o, on_ref[...]) * g_ref[...]


def _hgrn_step(q, lf, kin, v, gate, onorm, s0):
    db = q.shape[0]
    vec = pl.BlockSpec((None, 1, HG_D), lambda b, h: (b, 0, h))
    mat = pl.BlockSpec((None, None, HG_D, HG_D), lambda b, h: (b, h, 0, 0))
    return pl.pallas_call(
        _hgrn_step_kernel,
        grid=(db, HG_HEADS),
        in_specs=[vec] * 5 + [pl.BlockSpec((1, HG_D), lambda b, h: (0, 0)), mat],
        out_specs=[vec, mat],
        out_shape=[jax.ShapeDtypeStruct((db, 1, HG_W), F32), jax.ShapeDtypeStruct(s0.shape, F32)],
        compiler_params=_cp(("parallel", "parallel")),
        name="hgrn_step",
    )(q, lf, kin, v, gate, onorm, s0)


def _rope_tables(pos):
    half = HEAD_DIM // 2
    inv = ROPE_THETA ** (-jnp.arange(half, dtype=F32) / half)
    ang = pos.astype(F32)[:, None] * inv[None, :]
    cos = jnp.tile(jnp.cos(ang), (1, 2 * N_HEADS))
    sin = jnp.sin(ang)
    sin = jnp.tile(jnp.concatenate([-sin, sin], axis=1), (1, N_HEADS))
    return cos, sin


def _row_tile(m, want):
    return want if m % want == 0 else m


def kernel(x_prompt, x_sample, page_table, cache_fox_k, cache_fox_v, cache_fox_logf,
           cache_moba_k, cache_moba_v, state_hgrn, state_ffn_conv,
           norm_mix, w_in, fox_fb, fox_qnorm, fox_knorm, moba_qnorm, moba_knorm,
           hgrn_lb_logits, hgrn_onorm, w_branch_fox, w_branch_hgrn, w_branch_moba,
           w_out, norm_ffn, ffn_up, ffn_conv_w, ffn_conv_b, ffn_down):
    depth = w_in.shape[0]
    bp, tp, d = x_prompt.shape
    db, ts, _ = x_sample.shape
    n_pages = page_table.shape[1]
    past_len = n_pages * PAGE
    dff = ffn_down.shape[1]
    w = ATT_W

    sm = jax.nn.softmax(hgrn_lb_logits.astype(F32), axis=0)
    lower = jnp.clip(jnp.cumsum(sm, axis=0) - sm[0:1], 0.0, 0.999)

    head_of = np.arange(w) // HEAD_DIM
    gmat = jnp.asarray((head_of[:, None] == head_of[None, :]) / HEAD_DIM, BF16)

    cos_p, sin_p = _rope_tables(jnp.arange(tp))
    cos_s, sin_s = _rope_tables(jnp.full((db * ts,), past_len) + jnp.tile(jnp.arange(ts), db))
    half = HEAD_DIM // 2
    cos_pt, sin_pt = cos_p[:, :half].T, sin_p[:, half:HEAD_DIM].T

    to_pages = lambda a: jnp.transpose(a, (0, 1, 3, 4, 2))
    c_fk, c_fv, c_mk, c_mv = (to_pages(a) for a in (cache_fox_k, cache_fox_v,
                                                     cache_moba_k, cache_moba_v))
    c_lf = jnp.transpose(cache_fox_logf, (0, 1, 3, 2))

    xp = x_prompt.reshape(bp * tp, d)
    xs = x_sample.reshape(db * ts, d)
    mp, ms = xp.shape[0], xs.shape[0]
    new_p, new_s = [], []
    tile = lambda v: jnp.tile(v, N_HEADS)[None, :]
    wide = lambda v: jnp.broadcast_to(v[:, None], (HEAD_DIM, LANES))
    w_in_t = jnp.swapaxes(w_in, 1, 2)

    for l in range(depth):
        wl, wt = w_in[l], w_in_t[l]
        o_fq, o_fk, o_fv, o_ff = 0, w, 2 * w, 3 * w
        o_hg = o_ff + N_HEADS
        o_mq = o_hg + 4 * HG_W
        o_mk, o_mv, o_gate = o_mq + w, o_mq + 2 * w, o_mq + 3 * w
        wfox = wl[:, o_fq:o_ff].astype(BF16)
        wmoba = wl[:, o_mq:o_gate].astype(BF16)
        wq = jnp.concatenate([wl[:, o_fq:o_fk], wl[:, o_mq:o_mk]], axis=1).astype(BF16)
        wkv_t = jnp.concatenate([wt[o_fk:o_ff], wt[o_mk:o_gate]], axis=0).astype(BF16)
        wff = jnp.zeros((16, d), BF16).at[:N_HEADS].set(wt[o_ff:o_hg].astype(BF16))
        whg = wl[:, o_hg:o_mq].astype(BF16)
        wgate = wl[:, o_gate:].astype(BF16)
        fb = jnp.zeros((16, 1), F32).at[:N_HEADS, 0].set(fox_fb[l])
        gmix = norm_mix[l][None, :]
        lb = lower[l][None, :]
        onorm = hgrn_onorm[l][None, :]
        norms = (tile(fox_qnorm[l]), tile(fox_knorm[l]), tile(moba_qnorm[l]), tile(moba_knorm[l]))
        w_bf, w_bh, w_bm = (a[l].astype(BF16) for a in (w_branch_fox, w_branch_hgrn, w_branch_moba))
        w_o = w_out[l].astype(BF16)
        gffn = norm_ffn[l][None, :]
        up = ffn_up[l].astype(BF16)
        down = ffn_down[l].astype(BF16)
        cw = ffn_conv_w[l]
        cb = ffn_conv_b[l][None, :]

        tm = _row_tile(tp, 512)
        fq, mq, lft, fk_t, fv_t, mk_t, mv_t, fk_b, fv_b, mk_b, mv_b = _attn_proj_t(
            xp, gmix, wq, wkv_t, wff, fb, gmat, norms[0], norms[2], wide(fox_knorm[l]),
            wide(moba_knorm[l]), cos_p, sin_p, cos_pt, sin_pt, bp, tp, tm)
        hq, hlf, hkin, hi, hg = _hgrn_proj(xp, gmix, whg, lb, tm)
        c = _cumsum_rows(lft, tp)
        tk = _row_tile(tp, FLASH_CHUNK)
        o_f = _fox_prompt(fq, fk_b, fv_b, c, bp, tp, MOBA_BLOCK, tk)
        o_m = _moba_prompt(mq, mk_b, mv_b, _block_means(mk_t), bp, tp, tk)
        o_h, s_p = _hgrn_prompt(hq, hlf, hkin, hi, hg, onorm, bp, tp, _row_tile(tp, 512))
        xp = _merge(xp, gmix, wgate, o_f, o_h, o_m, w_bf, w_bh, w_bm, w_o, tm)
        xp, ta, tg = _ffn_prompt(xp, gffn, up, cw, cb, down, bp, tp, _row_tile(tp, 1024), 256)
        tiles = tp // _row_tile(tp, 1024)
        conv_p = jnp.concatenate([ta, tg], axis=-1)[tiles - 1::tiles, SUBLANES - (CONV_W - 1):]
        heads_last = lambda a: a.reshape(bp, N_HEADS, HEAD_DIM, tp).transpose(0, 3, 1, 2)
        new_p.append((heads_last(fk_t), heads_last(fv_t), lft[:N_HEADS].T.reshape(bp, tp, N_HEADS),
                      heads_last(mk_t), heads_last(mv_t), s_p, conv_p))

        fq, fk, fv, lft, mq, mk, mv = _attn_proj(xs, gmix, wfox, wff, fb, wmoba, gmat, *norms,
                                                 cos_s, sin_s, ms)
        hq, hlf, hkin, hi, hg = _hgrn_proj(xs, gmix, whg, lb, ms)
        r3 = lambda a: a.reshape(db, 1, a.shape[-1])
        hd1 = lambda a: a.reshape(db, N_HEADS, 1, HEAD_DIM)
        col = lambda a: jnp.broadcast_to(a[:, :, None], (db, w, PAGE))
        lf_new = jnp.broadcast_to(lft[:N_HEADS].T[:, :, None], (db, N_HEADS, PAGE))
        o_f, top = _decode_scan(page_table, l, col(fq), col(mq), lf_new, col(fk), col(fv),
                                c_fk, c_fv, c_lf, c_mk)
        o_f = o_f[:, :, 0]
        top_flat = top[:, :, :MOBA_TOPK].reshape(-1)
        o_m = _moba_step(page_table, top_flat, l, hd1(mq), hd1(mk), hd1(mv), c_mk, c_mv)
        o_h, s_s = _hgrn_step(r3(hq), r3(hlf), r3(hkin), r3(hi), r3(hg), onorm, state_hgrn[l])
        xs = _merge(xs, gmix, wgate, o_f.reshape(ms, w), o_h.reshape(ms, HG_W), o_m.reshape(ms, w),
                    w_bf, w_bh, w_bm, w_o, ms)
        prev = state_ffn_conv[l]
        xs, ua, ug = _ffn_step(xs, gffn, up, cw, cb, down, prev[:, 0], prev[:, 1], 256)
        conv_s = jnp.stack([prev[:, 1], jnp.concatenate([ua, ug], axis=-1)], axis=1)
        new_s.append((fk.reshape(db, ts, N_HEADS, HEAD_DIM), fv.reshape(db, ts, N_HEADS, HEAD_DIM),
                      lft[:N_HEADS].T.reshape(db, ts, N_HEADS),
                      mk.reshape(db, ts, N_HEADS, HEAD_DIM), mv.reshape(db, ts, N_HEADS, HEAD_DIM),
                      s_s, conv_s))

    outs_p = [jnp.stack(a) for a in zip(*new_p)]
    outs_s = [jnp.stack(a) for a in zip(*new_s)]
    return (xp.reshape(bp, tp, d), xs.reshape(db, ts, d), *outs_p, *outs_s)
```
